```python
import math
import jax, jax.numpy as jnp
from jax import lax
import numpy as np

D_MODEL = 1024
BATCH = 4
SEQ = 4096
DEPTH = 2

PLE_DIM = 256
BLOCK = 128
EPS = 1e-6
ROPE_THETA = 10000.0

MLA_HEADS = 8
MLA_Q_LORA = 384
MLA_KV_LORA = 256
MLA_NOPE = 64
MLA_ROPE = 32
MLA_V = 64
MLA_IN = MLA_Q_LORA + MLA_KV_LORA + MLA_ROPE
MLA_OUT = MLA_HEADS * MLA_V

DIFF_HEADS = 4
DIFF_QK = 64
DIFF_V = 2 * DIFF_QK
DIFF_QK_W = DIFF_HEADS * 2 * DIFF_QK
DIFF_IN = 2 * DIFF_QK_W + DIFF_HEADS * DIFF_V
DIFF_OUT = DIFF_HEADS * DIFF_V

SB_HEADS = 8
SB_DIM = 64
SB_W = SB_HEADS * SB_DIM
SB_IN = 3 * SB_W
SB_OUT = SB_W

D_IN = MLA_IN + DIFF_IN + SB_IN
N_BRANCH = 3
D_FF = 2816

kernel_name = "hybrid_mla_diff_stickbreak_macaron_ple"


def rms_norm(x, g):
    xf = x.astype(jnp.float32)
    y = xf * lax.rsqrt(jnp.mean(xf * xf, axis=-1, keepdims=True) + EPS)
    return (y * g.astype(jnp.float32)).astype(x.dtype)


def swiglu(x, w1, w3, w2):
    return (jax.nn.silu(x @ w1) * (x @ w3)) @ w2


def rope(x, pos):
    half = x.shape[-1] // 2
    freqs = 1.0 / (ROPE_THETA ** (jnp.arange(half, dtype=jnp.float32) / half))
    ang = pos.astype(jnp.float32)[:, None] * freqs[None, :]
    cos = jnp.cos(ang)[:, None, :]
    sin = jnp.sin(ang)[:, None, :]
    xf = x.astype(jnp.float32)
    x1, x2 = xf[..., :half], xf[..., half:]
    return jnp.concatenate([x1 * cos - x2 * sin, x2 * cos + x1 * sin], axis=-1).astype(x.dtype)


def to_blocks(t):
    b, s = t.shape[:2]
    t = t.reshape((b, s // BLOCK, BLOCK) + t.shape[2:])
    return jnp.moveaxis(t, 1, 0)


def from_blocks(t):
    t = jnp.moveaxis(t, 0, 1)
    return t.reshape((t.shape[0], t.shape[1] * t.shape[2]) + t.shape[3:])


def mla_branch(z, g_cq, g_ckv, w_uq, w_ukv, pos, blk_ids):
    b, s, _ = z.shape
    c_q = z[..., :MLA_Q_LORA]
    c_kv = z[..., MLA_Q_LORA:MLA_Q_LORA + MLA_KV_LORA]
    k_rope = z[..., MLA_Q_LORA + MLA_KV_LORA:]
    q = (rms_norm(c_q, g_cq) @ w_uq).reshape(b, s, MLA_HEADS, MLA_NOPE + MLA_ROPE)
    q_nope = q[..., :MLA_NOPE]
    q_rope = rope(q[..., MLA_NOPE:], pos)
    kv = (rms_norm(c_kv, g_ckv) @ w_ukv).reshape(b, s, MLA_HEADS, MLA_NOPE + MLA_V)
    k_nope, v = kv[..., :MLA_NOPE], kv[..., MLA_NOPE:]
    k_rope = rope(k_rope[:, :, None, :], pos)[:, :, 0, :]
    scale = (MLA_NOPE + MLA_ROPE) ** -0.5

    def block(args):
        qn, qr, blk = args
        qpos = blk * BLOCK + jnp.arange(BLOCK)
        sc = (jnp.einsum('bqhd,bkhd->bhqk', qn, k_nope)
              + jnp.einsum('bqhr,bkr->bhqk', qr, k_rope)).astype(jnp.float32) * scale
        sc = jnp.where(pos[None, :] <= qpos[:, None], sc, -jnp.inf)
        pr = jax.nn.softmax(sc, axis=-1).astype(v.dtype)
        return jnp.einsum('bhqk,bkhd->bqhd', pr, v)

    o = from_blocks(lax.map(block, (to_blocks(q_nope), to_blocks(q_rope), blk_ids)))
    return o.reshape(b, s, MLA_OUT)


def diff_branch(z, lq1, lk1, lq2, lk2, g_subln, lam_init, pos, blk_ids):
    b, s, _ = z.shape
    q = z[..., :DIFF_QK_W].reshape(b, s, DIFF_HEADS, 2, DIFF_QK)
    k = z[..., DIFF_QK_W:2 * DIFF_QK_W].reshape(b, s, DIFF_HEADS, 2, DIFF_QK)
    v = z[..., 2 * DIFF_QK_W:].reshape(b, s, DIFF_HEADS, DIFF_V)
    f32 = jnp.float32
    lam = (jnp.exp(jnp.sum(lq1.astype(f32) * lk1.astype(f32)))
           - jnp.exp(jnp.sum(lq2.astype(f32) * lk2.astype(f32))) + lam_init)
    head_idx = jnp.arange(1, DIFF_HEADS + 1, dtype=f32)
    slopes = jnp.exp2(-8.0 * head_idx / DIFF_HEADS)
    scale = DIFF_QK ** -0.5

    def block(args):
        qb, blk = args
        qpos = blk * BLOCK + jnp.arange(BLOCK)
        dist = (qpos[:, None] - pos[None, :]).astype(f32)
        bias = -slopes[:, None, None] * dist
        sc = jnp.einsum('bqhcd,bkhcd->bchqk', qb, k).astype(f32) * scale + bias
        sc = jnp.where(dist >= 0, sc, -jnp.inf)
        pr = jax.nn.softmax(sc, axis=-1)
        a = (pr[:, 0] - lam * pr[:, 1]).astype(v.dtype)
        return jnp.einsum('bhqk,bkhd->bqhd', a, v)

    o = from_blocks(lax.map(block, (to_blocks(q), blk_ids)))
    o = rms_norm(o, g_subln) * (1.0 - lam_init)
    return o.reshape(b, s, DIFF_OUT)


def stick_breaking_branch(z, pos, blk_ids):
    b, s, _ = z.shape
    q = z[..., :SB_W].reshape(b, s, SB_HEADS, SB_DIM)
    k = z[..., SB_W:2 * SB_W].reshape(b, s, SB_HEADS, SB_DIM)
    v = z[..., 2 * SB_W:].reshape(b, s, SB_HEADS, SB_DIM)
    scale = SB_DIM ** -0.5

    def block(args):
        qb, blk = args
        qpos = blk * BLOCK + jnp.arange(BLOCK)
        logits = jnp.einsum('bqhd,bkhd->bhqk', qb, k).astype(jnp.float32) * scale
        strict = pos[None, :] < qpos[:, None]
        log_beta = jax.nn.log_sigmoid(logits)
        log_1m = jnp.where(strict, jax.nn.log_sigmoid(-logits), 0.0)
        suffix = lax.cumsum(log_1m, axis=3, reverse=True) - log_1m
        w = jnp.where(strict, jnp.exp(log_beta + suffix), 0.0).astype(v.dtype)
        return jnp.einsum('bhqk,bkhd->bqhd', w, v)

    o = from_blocks(lax.map(block, (to_blocks(q), blk_ids)))
    return o.reshape(b, s, SB_OUT)


def hybrid_mixer(u, w_in, g_cq, g_ckv, w_uq, w_ukv, lq1, lk1, lq2, lk2, g_subln,
                 w_o_mla, w_o_diff, w_o_sb, w_branch_gate, w_out, lam_init):
    b, s, _ = u.shape
    pos = jnp.arange(s)
    blk_ids = jnp.arange(s // BLOCK)
    z = u @ w_in
    z_mla = z[..., :MLA_IN]
    z_diff = z[..., MLA_IN:MLA_IN + DIFF_IN]
    z_sb = z[..., MLA_IN + DIFF_IN:]
    y_mla = mla_branch(z_mla, g_cq, g_ckv, w_uq, w_ukv, pos, blk_ids)
    y_diff = diff_branch(z_diff, lq1, lk1, lq2, lk2, g_subln, lam_init, pos, blk_ids)
    y_sb = stick_breaking_branch(z_sb, pos, blk_ids)
    gates = jax.nn.sigmoid(u @ w_branch_gate).reshape(b, s, N_BRANCH, D_MODEL)
    merged = (gates[:, :, 0] * (y_mla @ w_o_mla)
              + gates[:, :, 1] * (y_diff @ w_o_diff)
              + gates[:, :, 2] * (y_sb @ w_o_sb))
    return merged @ w_out


def setup_inputs(seed: int = 0) -> dict:
    key = jax.random.key(seed)
    counter = [0]

    def nk():
        counter[0] += 1
        return jax.random.fold_in(key, counter[0])

    def dense(shape, fan_in, gain=1.0):
        return jax.random.normal(nk(), shape, jnp.float32) * (gain * fan_in ** -0.5)

    def norm_gain(shape):
        return 1.0 + 0.02 * jax.random.normal(nk(), shape, jnp.float32)

    def small(shape, std):
        return std * jax.random.normal(nk(), shape, jnp.float32)

    L = DEPTH
    return {
        "x": jax.random.normal(nk(), (BATCH, SEQ, D_MODEL), jnp.float32),
        "p": jax.random.normal(nk(), (DEPTH, BATCH, SEQ, PLE_DIM), jnp.float32),
        "g_ffn1": norm_gain((L, D_MODEL)),
        "w1_a": dense((L, D_MODEL, D_FF), D_MODEL),
        "w3_a": dense((L, D_MODEL, D_FF), D_MODEL),
        "w2_a": dense((L, D_FF, D_MODEL), D_FF),
        "g_mix": norm_gain((L, D_MODEL)),
        "w_in": dense((L, D_MODEL, D_IN), D_MODEL),
        "g_cq": norm_gain((L, MLA_Q_LORA)),
        "g_ckv": norm_gain((L, MLA_KV_LORA)),
        "w_uq": dense((L, MLA_Q_LORA, MLA_HEADS * (MLA_NOPE + MLA_ROPE)), MLA_Q_LORA),
        "w_ukv": dense((L, MLA_KV_LORA, MLA_HEADS * (MLA_NOPE + MLA_V)), MLA_KV_LORA),
        "lambda_q1": small((L, DIFF_QK), 0.1),
        "lambda_k1": small((L, DIFF_QK), 0.1),
        "lambda_q2": small((L, DIFF_QK), 0.1),
        "lambda_k2": small((L, DIFF_QK), 0.1),
        "g_subln": norm_gain((L, DIFF_V)),
        "w_o_mla": dense((L, MLA_OUT, D_MODEL), MLA_OUT),
        "w_o_diff": dense((L, DIFF_OUT, D_MODEL), DIFF_OUT),
        "w_o_sb": dense((L, SB_OUT, D_MODEL), SB_OUT),
        "w_branch_gate": dense((L, D_MODEL, N_BRANCH * D_MODEL), D_MODEL),
        "w_out": dense((L, D_MODEL, D_MODEL), D_MODEL),
        "g_ffn2": norm_gain((L, D_MODEL)),
        "w1_b": dense((L, D_MODEL, D_FF), D_MODEL),
        "w3_b": dense((L, D_MODEL, D_FF), D_MODEL),
        "w2_b": dense((L, D_FF, D_MODEL), D_FF),
        "g_ple": norm_gain((L, D_MODEL)),
        "w_ple_gate": dense((L, D_MODEL, D_MODEL), D_MODEL),
        "w_ple_proj": dense((L, PLE_DIM, D_MODEL), PLE_DIM),
        "g_final": norm_gain((D_MODEL,)),
    }


def reference(x, p, g_ffn1, w1_a, w3_a, w2_a, g_mix, w_in, g_cq, g_ckv, w_uq, w_ukv,
              lambda_q1, lambda_k1, lambda_q2, lambda_k2, g_subln, w_o_mla, w_o_diff, w_o_sb,
              w_branch_gate, w_out, g_ffn2, w1_b, w3_b, w2_b, g_ple, w_ple_gate, w_ple_proj,
              g_final):
    h = x
    for i in range(DEPTH):
        lam_init = 0.8 - 0.6 * math.exp(-0.3 * i)
        h = h + 0.5 * swiglu(rms_norm(h, g_ffn1[i]), w1_a[i], w3_a[i], w2_a[i])
        u = rms_norm(h, g_mix[i])
        h = h + hybrid_mixer(u, w_in[i], g_cq[i], g_ckv[i], w_uq[i], w_ukv[i],
                             lambda_q1[i], lambda_k1[i], lambda_q2[i], lambda_k2[i], g_subln[i],
                             w_o_mla[i], w_o_diff[i], w_o_sb[i], w_branch_gate[i], w_out[i],
                             lam_init)
        h = h + 0.5 * swiglu(rms_norm(h, g_ffn2[i]), w1_b[i], w3_b[i], w2_b[i])
        ple_gate = jax.nn.sigmoid(rms_norm(h, g_ple[i]) @ w_ple_gate[i])
        h = h + ple_gate * (p[i] @ w_ple_proj[i])
    return rms_norm(h, g_final)
```

```python
import functools
import math

import jax
import jax.numpy as jnp
import numpy as np
from jax import lax
from jax.experimental import pallas as pl
from jax.experimental.pallas import tpu as pltpu

D_MODEL = 1024
DEPTH = 2
PLE_DIM = 256
EPS = 1e-6
ROPE_THETA = 10000.0

MLA_HEADS = 8
MLA_Q_LORA = 384
MLA_KV_LORA = 256
MLA_NOPE = 64
MLA_ROPE = 32
MLA_V = 64
MLA_IN = MLA_Q_LORA + MLA_KV_LORA + MLA_ROPE

DIFF_HEADS = 4
DIFF_QK = 64
DIFF_V = 2 * DIFF_QK
DIFF_QK_W = DIFF_HEADS * 2 * DIFF_QK
DIFF_IN = 2 * DIFF_QK_W + DIFF_HEADS * DIFF_V

SB_HEADS = 8
SB_DIM = 64
SB_W = SB_HEADS * SB_DIM
SB_IN = 3 * SB_W

N_BRANCH = 3
D_FF = 2816

LANES = 128
HALF = LANES // 2
VMEM_LIMIT = 56 * 1024 * 1024
NEG = -1e30
SB_DEAD = -104.0

ROW_TILE = 512
PROJ_TILE = 256
ATT_BLOCK = 256

BF16 = jnp.bfloat16
F32 = jnp.float32


def _dot(a, b):
    return jnp.dot(a, b, preferred_element_type=F32)


def _dot_nt(a, b):
    return lax.dot_general(a, b, (((1,), (1,)), ((), ())), preferred_element_type=F32)


def _rms(x, g):
    ms = jnp.mean(x * x, axis=-1, keepdims=True)
    return x * lax.rsqrt(ms + EPS) * g


def _sigmoid(x):
    return 1.0 / (1.0 + jnp.exp(-x))


def _resident(shape):
    nd = len(shape)
    return pl.BlockSpec(shape, lambda *_: (0,) * nd, pipeline_mode=pl.Buffered(1))


def _params(sem):
    return pltpu.CompilerParams(dimension_semantics=sem, vmem_limit_bytes=VMEM_LIMIT)


def _ffn_kernel(h_ref, g_ref, w1_ref, w3_ref, w2_ref, o_ref):
    x = h_ref[...]
    n = _rms(x, g_ref[...]).astype(BF16)
    a = _dot(n, w1_ref[...])
    b = _dot(n, w3_ref[...])
    act = (a * _sigmoid(a) * b).astype(BF16)
    o_ref[...] = x + 0.5 * _dot(act, w2_ref[...])


def _ffn(h, g, w1, w3, w2):
    t = h.shape[0]
    row = pl.BlockSpec((ROW_TILE, D_MODEL), lambda i: (i, 0))
    return pl.pallas_call(
        _ffn_kernel,
        grid=(t // ROW_TILE,),
        in_specs=[row, _resident((1, D_MODEL)), _resident(w1.shape), _resident(w3.shape),
                  _resident(w2.shape)],
        out_specs=row,
        out_shape=jax.ShapeDtypeStruct(h.shape, F32),
        compiler_params=_params(("parallel",)),
        name="ffn",
    )(h, g, w1, w3, w2)


def _rope(x, cos, sin_a, sin_b):
    return (x * cos + pltpu.roll(x, LANES - MLA_ROPE // 2, 1) * sin_a
            + pltpu.roll(x, MLA_ROPE // 2, 1) * sin_b)


def _proj_kernel(h_ref, g_ref, wc_ref, wkr_ref, wd_ref, ws_ref, gcq_ref, gckv_ref, wuq_ref,
                 wuk_ref, wuv_ref, cos_ref, sa_ref, sb_ref,
                 qm_ref, km_ref, vm_ref, qd_ref, kd_ref, vd_ref, qs_ref, ks_ref, vs_ref):
    u = _rms(h_ref[0], g_ref[...]).astype(BF16)
    cos, sin_a, sin_b = cos_ref[...], sa_ref[...], sb_ref[...]

    c = _dot(u, wc_ref[...])
    cq = _rms(c[:, :MLA_Q_LORA], gcq_ref[...]).astype(BF16)
    ckv = _rms(c[:, MLA_Q_LORA:], gckv_ref[...]).astype(BF16)
    q = _dot(cq, wuq_ref[...])
    kn = _dot(ckv, wuk_ref[...])
    v = _dot(ckv, wuv_ref[...])
    kr = _rope(_dot(u, wkr_ref[...]), cos, sin_a, sin_b)
    for hd in range(MLA_HEADS):
        sl = slice(hd * LANES, (hd + 1) * LANES)
        qm_ref[0, hd] = _rope(q[:, sl], cos, sin_a, sin_b).astype(BF16)
        km_ref[0, hd] = (kn[:, sl] + kr).astype(BF16)
    for p in range(MLA_HEADS // 2):
        vm_ref[0, p] = v[:, p * LANES:(p + 1) * LANES].astype(BF16)

    d = _dot(u, wd_ref[...])
    for hd in range(DIFF_HEADS):
        sl = slice(hd * LANES, (hd + 1) * LANES)
        qd_ref[0, hd] = (d[:, sl] * DIFF_QK ** -0.5).astype(BF16)
        kd_ref[0, hd] = d[:, DIFF_QK_W + hd * LANES:DIFF_QK_W + (hd + 1) * LANES].astype(BF16)
        vd_ref[0, hd] = d[:, 2 * DIFF_QK_W + hd * LANES:2 * DIFF_QK_W + (hd + 1) * LANES].astype(BF16)

    s = _dot(u, ws_ref[...])
    for p in range(SB_HEADS // 2):
        sl = slice(p * LANES, (p + 1) * LANES)
        qs_ref[0, p] = (s[:, sl] * SB_DIM ** -0.5).astype(BF16)
        ks_ref[0, p] = s[:, SB_W + p * LANES:SB_W + (p + 1) * LANES].astype(BF16)
        vs_ref[0, p] = s[:, 2 * SB_W + p * LANES:2 * SB_W + (p + 1) * LANES].astype(BF16)


def _proj(h3, g, wc, wkr, wd, ws, gcq, gckv, wuq, wuk, wuv, cos, sin_a, sin_b):
    b, s, _ = h3.shape
    tab = pl.BlockSpec((PROJ_TILE, LANES), lambda bi, si: (si, 0))

    def heads(n):
        return pl.BlockSpec((1, n, PROJ_TILE, LANES), lambda bi, si: (bi, 0, si, 0))

    def out(n):
        return jax.ShapeDtypeStruct((b, n, s, LANES), BF16)

    n_out = [MLA_HEADS, MLA_HEADS, MLA_HEADS // 2, DIFF_HEADS, DIFF_HEADS, DIFF_HEADS,
             SB_HEADS // 2, SB_HEADS // 2, SB_HEADS // 2]
    return pl.pallas_call(
        _proj_kernel,
        grid=(b, s // PROJ_TILE),
        in_specs=[pl.BlockSpec((1, PROJ_TILE, D_MODEL), lambda bi, si: (bi, si, 0)),
                  _resident(g.shape), _resident(wc.shape), _resident(wkr.shape),
                  _resident(wd.shape), _resident(ws.shape), _resident(gcq.shape),
                  _resident(gckv.shape), _resident(wuq.shape), _resident(wuk.shape),
                  _resident(wuv.shape), tab, tab, tab],
        out_specs=[heads(n) for n in n_out],
        out_shape=[out(n) for n in n_out],
        compiler_params=_params(("parallel", "parallel")),
        name="proj",
    )(h3, g, wc, wkr, wd, ws, gcq, gckv, wuq, wuk, wuv, cos, sin_a, sin_b)


def _merge_kernel(h_ref, g_ref, ya_ref, yb_ref, yc_ref, wg_ref, woa_ref, wob_ref, woc_ref,
                  wout_ref, o_ref):
    x = h_ref[...]
    u = _rms(x, g_ref[...]).astype(BF16)
    merged = None
    for i, (y_ref, wo_ref) in enumerate(((ya_ref, woa_ref), (yb_ref, wob_ref), (yc_ref, woc_ref))):
        gate = _sigmoid(_dot(u, wg_ref[:, i * D_MODEL:(i + 1) * D_MODEL]))
        term = gate * _dot(y_ref[...], wo_ref[...])
        merged = term if merged is None else merged + term
    o_ref[...] = x + _dot(merged.astype(BF16), wout_ref[...])


def _merge(h, g, ya, yb, yc, wg, woa, wob, woc, wout):
    t = h.shape[0]
    row = pl.BlockSpec((ROW_TILE, D_MODEL), lambda i: (i, 0))
    yrow = pl.BlockSpec((ROW_TILE, ya.shape[1]), lambda i: (i, 0))
    return pl.pallas_call(
        _merge_kernel,
        grid=(t // ROW_TILE,),
        in_specs=[row, _resident(g.shape), yrow, yrow, yrow, _resident(wg.shape),
                  _resident(woa.shape), _resident(wob.shape), _resident(woc.shape),
                  _resident(wout.shape)],
        out_specs=row,
        out_shape=jax.ShapeDtypeStruct(h.shape, F32),
        compiler_params=_params(("parallel",)),
        name="merge",
    )(h, g, ya, yb, yc, wg, woa, wob, woc, wout)


def _ple_kernel(h_ref, g_ref, p_ref, wpg_ref, wpe_ref, gfin_ref, o_ref, *, final_norm):
    x = h_ref[...]
    gate = _sigmoid(_dot(_rms(x, g_ref[...]).astype(BF16), wpg_ref[...]))
    y = x + gate * _dot(p_ref[...].astype(BF16), wpe_ref[...])
    if final_norm:
        y = _rms(y, gfin_ref[...])
    o_ref[...] = y


def _ple(h, g, p, wpg, wpe, gfin, final_norm):
    t = h.shape[0]
    row = pl.BlockSpec((ROW_TILE, D_MODEL), lambda i: (i, 0))
    return pl.pallas_call(
        functools.partial(_ple_kernel, final_norm=final_norm),
        grid=(t // ROW_TILE,),
        in_specs=[row, _resident(g.shape), pl.BlockSpec((ROW_TILE, PLE_DIM), lambda i: (i, 0)),
                  _resident(wpg.shape), _resident(wpe.shape), _resident(gfin.shape)],
        out_specs=row,
        out_shape=jax.ShapeDtypeStruct(h.shape, F32),
        compiler_params=_params(("parallel",)),
        name="ple",
    )(h, g, p, wpg, wpe, gfin)


def _causal_iotas(rows):
    r = jnp.bitwise_and(lax.broadcasted_iota(jnp.int32, (rows, ATT_BLOCK), 0), ATT_BLOCK - 1)
    c = lax.broadcasted_iota(jnp.int32, (rows, ATT_BLOCK), 1)
    return r, c


def _softmax_step(s, v, m_ref, l_ref, acc_ref):
    m_old = m_ref[...]
    m_new = jnp.maximum(m_old, jnp.max(s, axis=-1, keepdims=True))
    alpha = jnp.exp(m_old - m_new)
    p = jnp.exp(s - m_new)
    l_ref[...] = alpha * l_ref[...] + jnp.sum(p, axis=-1, keepdims=True)
    acc_ref[...] = alpha * acc_ref[...] + _dot(p.astype(BF16), v)
    m_ref[...] = m_new


def _softmax_init(m_ref, l_ref, acc_ref):
    m_ref[...] = jnp.full(m_ref.shape, NEG, F32)
    l_ref[...] = jnp.zeros(l_ref.shape, F32)
    acc_ref[...] = jnp.zeros(acc_ref.shape, F32)


def _lane_is_low(shape):
    return lax.broadcasted_iota(jnp.int32, shape, len(shape) - 1) < HALF


def _mla_kernel(q_ref, k_ref, v_ref, o_ref, m_ref, l_ref, acc_ref):
    qi = pl.program_id(2)
    scale = (MLA_NOPE + MLA_ROPE) ** -0.5
    q_a, q_b = q_ref[0, 0], q_ref[0, 1]
    _softmax_init(m_ref, l_ref, acc_ref)

    def scores(kb):
        ks = pl.ds(pl.multiple_of(kb * ATT_BLOCK, ATT_BLOCK), ATT_BLOCK)
        s = jnp.concatenate([_dot_nt(q_a, k_ref[0, 0, ks, :]), _dot_nt(q_b, k_ref[0, 1, ks, :])], axis=0)
        return s * scale, v_ref[0, 0, ks, :]

    def body(kb, carry):
        s, v = scores(kb)
        _softmax_step(s, v, m_ref, l_ref, acc_ref)
        return carry

    lax.fori_loop(0, qi, body, 0)
    s, v = scores(qi)
    r, c = _causal_iotas(2 * ATT_BLOCK)
    _softmax_step(jnp.where(c <= r, s, NEG), v, m_ref, l_ref, acc_ref)

    o = acc_ref[...] / l_ref[...]
    o_ref[0] = jnp.where(_lane_is_low((ATT_BLOCK, LANES)), o[:ATT_BLOCK], o[ATT_BLOCK:]).astype(BF16)


def _att_specs(n_q_heads, n_k_heads, seq):
    q = pl.BlockSpec((1, n_q_heads, ATT_BLOCK, LANES), lambda b, g, i: (b, g, i, 0))
    k = pl.BlockSpec((1, n_k_heads, seq, LANES), lambda b, g, i: (b, g, 0, 0))
    v = pl.BlockSpec((1, 1, seq, LANES), lambda b, g, i: (b, g, 0, 0))
    o = pl.BlockSpec((1, ATT_BLOCK, LANES), lambda b, g, i: (b, i, g))
    return q, k, v, o


def _att_scratch():
    rows = 2 * ATT_BLOCK
    return [pltpu.VMEM((rows, 1), F32), pltpu.VMEM((rows, 1), F32), pltpu.VMEM((rows, LANES), F32)]


def _mla(q, k, v):
    b, _, s, _ = q.shape
    groups = MLA_HEADS // 2
    qs, ks, vs, os_ = _att_specs(2, 2, s)
    return pl.pallas_call(
        _mla_kernel,
        grid=(b, groups, s // ATT_BLOCK),
        in_specs=[qs, ks, vs],
        out_specs=os_,
        out_shape=jax.ShapeDtypeStruct((b, s, groups * LANES), BF16),
        scratch_shapes=_att_scratch(),
        compiler_params=_params(("parallel", "parallel", "arbitrary")),
        name="mla",
    )(q, k, v)


def _diff_kernel(q_ref, k_ref, v_ref, slope_ref, lam_ref, gsub_ref, o_ref, m_ref, l_ref, acc_ref,
                 *, lam_init):
    qi = pl.program_id(2)
    q = q_ref[0, 0]
    low = _lane_is_low(q.shape)
    zero = jnp.zeros_like(q)
    qq = jnp.concatenate([jnp.where(low, q, zero), jnp.where(low, zero, q)], axis=0)
    slope = slope_ref[0]
    slope = slope[:, :1]
    r, c = _causal_iotas(2 * ATT_BLOCK)
    rel = (c - r).astype(F32)
    _softmax_init(m_ref, l_ref, acc_ref)

    def scores(kb):
        ks = pl.ds(pl.multiple_of(kb * ATT_BLOCK, ATT_BLOCK), ATT_BLOCK)
        off = ((kb - qi) * ATT_BLOCK).astype(F32)
        s = _dot_nt(qq, k_ref[0, 0, ks, :]) + slope * (rel + off)
        return s, v_ref[0, 0, ks, :]

    def body(kb, carry):
        s, v = scores(kb)
        _softmax_step(s, v, m_ref, l_ref, acc_ref)
        return carry

    lax.fori_loop(0, qi, body, 0)
    s, v = scores(qi)
    _softmax_step(jnp.where(c <= r, s, NEG), v, m_ref, l_ref, acc_ref)

    lp = lam_ref[...]
    lam = (jnp.exp(jnp.sum(lp[0:1] * lp[1:2], axis=-1, keepdims=True))
           - jnp.exp(jnp.sum(lp[2:3] * lp[3:4], axis=-1, keepdims=True)) + lam_init)
    o = acc_ref[...] / l_ref[...]
    o = o[:ATT_BLOCK] - lam * o[ATT_BLOCK:]
    o_ref[0] = (_rms(o, gsub_ref[...]) * (1.0 - lam_init)).astype(BF16)


def _diff(q, k, v, slopes, lam_p, gsub, lam_init):
    b, _, s, _ = q.shape
    qs, ks, vs, os_ = _att_specs(1, 1, s)
    return pl.pallas_call(
        functools.partial(_diff_kernel, lam_init=lam_init),
        grid=(b, DIFF_HEADS, s // ATT_BLOCK),
        in_specs=[qs, ks, vs, pl.BlockSpec((1, 1, LANES), lambda b_, g, i: (g, 0, 0)),
                  pl.BlockSpec(lam_p.shape, lambda b_, g, i: (0, 0)),
                  pl.BlockSpec(gsub.shape, lambda b_, g, i: (0, 0))],
        out_specs=os_,
        out_shape=jax.ShapeDtypeStruct((b, s, DIFF_HEADS * LANES), BF16),
        scratch_shapes=_att_scratch(),
        compiler_params=_params(("parallel", "parallel", "arbitrary")),
        name="diff",
    )(q, k, v, slopes, lam_p, gsub)


def _sb_kernel(q_ref, k_ref, v_ref, o_ref, run_ref, acc_ref):
    qi = pl.program_id(2)
    q = q_ref[0, 0]
    low = _lane_is_low(q.shape)
    zero = jnp.zeros_like(q)
    qq = jnp.concatenate([jnp.where(low, q, zero), jnp.where(low, zero, q)], axis=0)
    r, c = _causal_iotas(2 * ATT_BLOCK)
    strict = c < r
    kr = lax.broadcasted_iota(jnp.int32, (ATT_BLOCK, ATT_BLOCK), 0)
    kc = lax.broadcasted_iota(jnp.int32, (ATT_BLOCK, ATT_BLOCK), 1)
    later = (kr > kc).astype(BF16)
    run_ref[...] = jnp.zeros(run_ref.shape, F32)
    acc_ref[...] = jnp.zeros(acc_ref.shape, F32)

    def step(kb, diagonal):
        ks = pl.ds(pl.multiple_of(kb * ATT_BLOCK, ATT_BLOCK), ATT_BLOCK)
        s = _dot_nt(qq, k_ref[0, 0, ks, :])
        soft = jnp.log(1.0 + jnp.exp(-jnp.abs(s)))
        log_beta = jnp.minimum(s, 0.0) - soft
        log_1m = log_beta - s
        if diagonal:
            log_1m = jnp.where(strict, log_1m, 0.0)
        hi = log_1m.astype(BF16)
        lo = (log_1m - hi.astype(F32)).astype(BF16)
        suffix = _dot(hi, later) + _dot(lo, later)
        run = run_ref[...]
        w = jnp.exp(log_beta + suffix + run)
        if diagonal:
            w = jnp.where(strict, w, 0.0)
        acc_ref[...] += _dot(w.astype(BF16), v_ref[0, 0, ks, :])
        run = run + jnp.sum(log_1m, axis=-1, keepdims=True)
        run_ref[...] = run
        return jnp.max(run)

    top = step(qi, True)

    def cond(carry):
        kb, top = carry
        return jnp.logical_and(kb >= 0, top > SB_DEAD)

    def body(carry):
        kb, _ = carry
        return kb - 1, step(kb, False)

    lax.while_loop(cond, body, (qi - 1, top))

    o = acc_ref[...]
    o_ref[0] = jnp.where(_lane_is_low((ATT_BLOCK, LANES)), o[:ATT_BLOCK], o[ATT_BLOCK:]).astype(BF16)


def _sb(q, k, v):
    b, groups, s, _ = q.shape
    qs, ks, vs, os_ = _att_specs(1, 1, s)
    rows = 2 * ATT_BLOCK
    return pl.pallas_call(
        _sb_kernel,
        grid=(b, groups, s // ATT_BLOCK),
        in_specs=[qs, ks, vs],
        out_specs=os_,
        out_shape=jax.ShapeDtypeStruct((b, s, groups * LANES), BF16),
        scratch_shapes=[pltpu.VMEM((rows, 1), F32), pltpu.VMEM((rows, LANES), F32)],
        compiler_params=_params(("parallel", "parallel", "arbitrary")),
        name="sb",
    )(q, k, v)


def _rope_tables(seq):
    half = MLA_ROPE // 2
    freqs = 1.0 / (ROPE_THETA ** (jnp.arange(half, dtype=F32) / half))
    ang = jnp.arange(seq, dtype=F32)[:, None] * freqs[None, :]
    cos, sin = jnp.cos(ang), jnp.sin(ang)
    one = jnp.ones((seq, MLA_NOPE), F32)
    zn = jnp.zeros((seq, MLA_NOPE), F32)
    zh = jnp.zeros((seq, half), F32)
    zt = jnp.zeros((seq, LANES - MLA_NOPE - MLA_ROPE), F32)
    cos_t = jnp.concatenate([one, cos, cos, zt], axis=1)
    sin_a = jnp.concatenate([zn, -sin, zh, zt], axis=1)
    sin_b = jnp.concatenate([zn, zh, sin, zt], axis=1)
    return cos_t, sin_a, sin_b


def _layer_weights(w_in, w_uq, w_ukv):
    pad = LANES - MLA_NOPE - MLA_ROPE
    wc = w_in[:, :MLA_Q_LORA + MLA_KV_LORA]
    wkr = jnp.pad(w_in[:, MLA_Q_LORA + MLA_KV_LORA:MLA_IN], ((0, 0), (MLA_NOPE, pad)))
    wd = w_in[:, MLA_IN:MLA_IN + DIFF_IN]
    ws = w_in[:, MLA_IN + DIFF_IN:]
    wuq = jnp.pad(w_uq.reshape(MLA_Q_LORA, MLA_HEADS, MLA_NOPE + MLA_ROPE),
                  ((0, 0), (0, 0), (0, pad))).reshape(MLA_Q_LORA, MLA_HEADS * LANES)
    kv = w_ukv.reshape(MLA_KV_LORA, MLA_HEADS, MLA_NOPE + MLA_V)
    wuk = jnp.pad(kv[:, :, :MLA_NOPE], ((0, 0), (0, 0), (0, LANES - MLA_NOPE))).reshape(
        MLA_KV_LORA, MLA_HEADS * LANES)
    wuv = kv[:, :, MLA_NOPE:].reshape(MLA_KV_LORA, MLA_HEADS * MLA_V)
    return [w.astype(BF16) for w in (wc, wkr, wd, ws, wuq, wuk, wuv)]


def kernel(x, p, g_ffn1, w1_a, w3_a, w2_a, g_mix, w_in, g_cq, g_ckv, w_uq, w_ukv, lambda_q1, lambda_k1, lambda_q2, lambda_k2, g_subln, w_o_mla, w_o_diff, w_o_sb, w_branch_gate, w_out, g_ffn2, w1_b, w3_b, w2_b, g_ple, w_ple_gate, w_ple_proj, g_final):
    b, s, d = x.shape
    t = b * s
    assert d == D_MODEL and t % ROW_TILE == 0 and s % PROJ_TILE == 0 and s % ATT_BLOCK == 0
    cos_t, sin_a, sin_b = _rope_tables(s)
    slopes = np.exp2(-8.0 * np.arange(1, DIFF_HEADS + 1, dtype=np.float32) / DIFF_HEADS)
    slopes = jnp.asarray(np.broadcast_to(slopes[:, None, None], (DIFF_HEADS, 1, LANES)), F32)
    row = lambda v: v.reshape(1, -1)
    bf = lambda w: w.astype(BF16)

    h = x.reshape(t, d)
    for i in range(DEPTH):
        lam_init = 0.8 - 0.6 * math.exp(-0.3 * i)
        h = _ffn(h, row(g_ffn1[i]), bf(w1_a[i]), bf(w3_a[i]), bf(w2_a[i]))
        wc, wkr, wd, ws, wuq, wuk, wuv = _layer_weights(w_in[i], w_uq[i], w_ukv[i])
        qm, km, vm, qd, kd, vd, qs, ks, vs = _proj(
            h.reshape(b, s, d), row(g_mix[i]), wc, wkr, wd, ws, row(g_cq[i]), row(g_ckv[i]),
            wuq, wuk, wuv, cos_t, sin_a, sin_b)
        y_mla = _mla(qm, km, vm).reshape(t, -1)
        lam_p = jnp.stack([lambda_q1[i], lambda_k1[i], lambda_q2[i], lambda_k2[i]])
        y_diff = _diff(qd, kd, vd, slopes, lam_p, row(g_subln[i]), lam_init).reshape(t, -1)
        y_sb = _sb(qs, ks, vs).reshape(t, -1)
        h = _merge(h, row(g_mix[i]), y_mla, y_diff, y_sb, bf(w_branch_gate[i]), bf(w_o_mla[i]),
                   bf(w_o_diff[i]), bf(w_o_sb[i]), bf(w_out[i]))
        h = _ffn(h, row(g_ffn2[i]), bf(w1_b[i]), bf(w3_b[i]), bf(w2_b[i]))
        h = _ple(h, row(g_ple[i]), p[i].reshape(t, PLE_DIM), bf(w_ple_gate[i]), bf(w_ple_proj[i]),
                 row(g_final), final_norm=(i == DEPTH - 1))
    return h.reshape(b, s, d)
```

```python
import functools
import math

import jax
import jax.numpy as jnp
import numpy as np
from jax import lax
from jax.experimental import pallas as pl
from jax.experimental.pallas import tpu as pltpu

D_MODEL = 1024
DEPTH = 2
PLE_DIM = 256
EPS = 1e-6
ROPE_THETA = 10000.0

MLA_HEADS = 8
MLA_Q_LORA = 384
MLA_KV_LORA = 256
MLA_NOPE = 64
MLA_ROPE = 32
MLA_V = 64
MLA_IN = MLA_Q_LORA + MLA_KV_LORA + MLA_ROPE

DIFF_HEADS = 4
DIFF_QK = 64
DIFF_V = 2 * DIFF_QK
DIFF_QK_W = DIFF_HEADS * 2 * DIFF_QK
DIFF_IN = 2 * DIFF_QK_W + DIFF_HEADS * DIFF_V

SB_HEADS = 8
SB_DIM = 64
SB_W = SB_HEADS * SB_DIM
SB_IN = 3 * SB_W

N_BRANCH = 3
D_FF = 2816

LANES = 128
HALF = LANES // 2
VMEM_LIMIT = 56 * 1024 * 1024
NEG = -1e30
SB_DEAD = -104.0

ROW_TILE = 512
PROJ_TILE = 256
ATT_BLOCK = 512
SB_BLOCK = 256
LOG2E = math.log2(math.e)

BF16 = jnp.bfloat16
F32 = jnp.float32


def _dot(a, b):
    return jnp.dot(a, b, preferred_element_type=F32)


def _dot_nt(a, b):
    return lax.dot_general(a, b, (((1,), (1,)), ((), ())), preferred_element_type=F32)


def _rms(x, g):
    ms = jnp.mean(x * x, axis=-1, keepdims=True)
    return x * lax.rsqrt(ms + EPS) * g


def _sigmoid(x):
    return 1.0 / (1.0 + jnp.exp(-x))


def _resident(shape):
    nd = len(shape)
    return pl.BlockSpec(shape, lambda *_: (0,) * nd, pipeline_mode=pl.Buffered(1))


def _params(sem):
    return pltpu.CompilerParams(dimension_semantics=sem, vmem_limit_bytes=VMEM_LIMIT)


def _ffn_kernel(h_ref, g_ref, w1_ref, w3_ref, w2_ref, o_ref):
    x = h_ref[...]
    n = _rms(x, g_ref[...]).astype(BF16)
    a = _dot(n, w1_ref[...])
    b = _dot(n, w3_ref[...])
    act = (a * _sigmoid(a) * b).astype(BF16)
    o_ref[...] = x + 0.5 * _dot(act, w2_ref[...])


def _ffn(h, g, w1, w3, w2):
    t = h.shape[0]
    row = pl.BlockSpec((ROW_TILE, D_MODEL), lambda i: (i, 0))
    return pl.pallas_call(
        _ffn_kernel,
        grid=(t // ROW_TILE,),
        in_specs=[row, _resident((1, D_MODEL)), _resident(w1.shape), _resident(w3.shape),
                  _resident(w2.shape)],
        out_specs=row,
        out_shape=jax.ShapeDtypeStruct(h.shape, F32),
        compiler_params=_params(("parallel",)),
        name="ffn",
    )(h, g, w1, w3, w2)


def _rope(x, cos, sin_a, sin_b):
    return (x * cos + pltpu.roll(x, LANES - MLA_ROPE // 2, 1) * sin_a
            + pltpu.roll(x, MLA_ROPE // 2, 1) * sin_b)


def _proj_kernel(h_ref, g_ref, wc_ref, wkr_ref, wd_ref, ws_ref, gcq_ref, gckv_ref, wuq_ref,
                 wuk_ref, wuv_ref, cos_ref, sa_ref, sb_ref,
                 qm_ref, km_ref, vm_ref, qd_ref, kd_ref, vd_ref, qs_ref, ks_ref, vs_ref):
    u = _rms(h_ref[0], g_ref[...]).astype(BF16)
    cos, sin_a, sin_b = cos_ref[...], sa_ref[...], sb_ref[...]

    c = _dot(u, wc_ref[...])
    cq = _rms(c[:, :MLA_Q_LORA], gcq_ref[...]).astype(BF16)
    ckv = _rms(c[:, MLA_Q_LORA:], gckv_ref[...]).astype(BF16)
    q = _dot(cq, wuq_ref[...])
    kn = _dot(ckv, wuk_ref[...])
    v = _dot(ckv, wuv_ref[...])
    kr = _rope(_dot(u, wkr_ref[...]), cos, sin_a, sin_b)
    for hd in range(MLA_HEADS):
        sl = slice(hd * LANES, (hd + 1) * LANES)
        qm_ref[0, hd] = _rope(q[:, sl], cos, sin_a, sin_b).astype(BF16)
        km_ref[0, hd] = (kn[:, sl] + kr).astype(BF16)
    for p in range(MLA_HEADS // 2):
        vm_ref[0, p] = v[:, p * LANES:(p + 1) * LANES].astype(BF16)

    d = _dot(u, wd_ref[...])
    for hd in range(DIFF_HEADS):
        sl = slice(hd * LANES, (hd + 1) * LANES)
        qd_ref[0, hd] = (d[:, sl] * DIFF_QK ** -0.5).astype(BF16)
        kd_ref[0, hd] = d[:, DIFF_QK_W + hd * LANES:DIFF_QK_W + (hd + 1) * LANES].astype(BF16)
        vd_ref[0, hd] = d[:, 2 * DIFF_QK_W + hd * LANES:2 * DIFF_QK_W + (hd + 1) * LANES].astype(BF16)

    s = _dot(u, ws_ref[...])
    for p in range(SB_HEADS // 2):
        sl = slice(p * LANES, (p + 1) * LANES)
        qs_ref[0, p] = (s[:, sl] * SB_DIM ** -0.5).astype(BF16)
        ks_ref[0, p] = s[:, SB_W + p * LANES:SB_W + (p + 1) * LANES].astype(BF16)
        vs_ref[0, p] = s[:, 2 * SB_W + p * LANES:2 * SB_W + (p + 1) * LANES].astype(BF16)


def _proj(h3, g, wc, wkr, wd, ws, gcq, gckv, wuq, wuk, wuv, cos, sin_a, sin_b):
    b, s, _ = h3.shape
    tab = pl.BlockSpec((PROJ_TILE, LANES), lambda bi, si: (si, 0))

    def heads(n):
        return pl.BlockSpec((1, n, PROJ_TILE, LANES), lambda bi, si: (bi, 0, si, 0))

    def out(n):
        return jax.ShapeDtypeStruct((b, n, s, LANES), BF16)

    n_out = [MLA_HEADS, MLA_HEADS, MLA_HEADS // 2, DIFF_HEADS, DIFF_HEADS, DIFF_HEADS,
             SB_HEADS // 2, SB_HEADS // 2, SB_HEADS // 2]
    return pl.pallas_call(
        _proj_kernel,
        grid=(b, s // PROJ_TILE),
        in_specs=[pl.BlockSpec((1, PROJ_TILE, D_MODEL), lambda bi, si: (bi, si, 0)),
                  _resident(g.shape), _resident(wc.shape), _resident(wkr.shape),
                  _resident(wd.shape), _resident(ws.shape), _resident(gcq.shape),
                  _resident(gckv.shape), _resident(wuq.shape), _resident(wuk.shape),
                  _resident(wuv.shape), tab, tab, tab],
        out_specs=[heads(n) for n in n_out],
        out_shape=[out(n) for n in n_out],
        compiler_params=_params(("parallel", "parallel")),
        name="proj",
    )(h3, g, wc, wkr, wd, ws, gcq, gckv, wuq, wuk, wuv, cos, sin_a, sin_b)


def _merge_kernel(h_ref, g_ref, ya_ref, yb_ref, yc_ref, wg_ref, woa_ref, wob_ref, woc_ref,
                  wout_ref, o_ref):
    x = h_ref[...]
    u = _rms(x, g_ref[...]).astype(BF16)
    merged = None
    for i, (y_ref, wo_ref) in enumerate(((ya_ref, woa_ref), (yb_ref, wob_ref), (yc_ref, woc_ref))):
        gate = _sigmoid(_dot(u, wg_ref[:, i * D_MODEL:(i + 1) * D_MODEL]))
        term = gate * _dot(y_ref[...], wo_ref[...])
        merged = term if merged is None else merged + term
    o_ref[...] = x + _dot(merged.astype(BF16), wout_ref[...])


def _merge(h, g, ya, yb, yc, wg, woa, wob, woc, wout):
    t = h.shape[0]
    row = pl.BlockSpec((ROW_TILE, D_MODEL), lambda i: (i, 0))
    yrow = pl.BlockSpec((ROW_TILE, ya.shape[1]), lambda i: (i, 0))
    return pl.pallas_call(
        _merge_kernel,
        grid=(t // ROW_TILE,),
        in_specs=[row, _resident(g.shape), yrow, yrow, yrow, _resident(wg.shape),
                  _resident(woa.shape), _resident(wob.shape), _resident(woc.shape),
                  _resident(wout.shape)],
        out_specs=row,
        out_shape=jax.ShapeDtypeStruct(h.shape, F32),
        compiler_params=_params(("parallel",)),
        name="merge",
    )(h, g, ya, yb, yc, wg, woa, wob, woc, wout)


def _ple_kernel(h_ref, g_ref, p_ref, wpg_ref, wpe_ref, gfin_ref, o_ref, *, final_norm):
    x = h_ref[...]
    gate = _sigmoid(_dot(_rms(x, g_ref[...]).astype(BF16), wpg_ref[...]))
    y = x + gate * _dot(p_ref[...].astype(BF16), wpe_ref[...])
    if final_norm:
        y = _rms(y, gfin_ref[...])
    o_ref[...] = y


def _ple(h, g, p, wpg, wpe, gfin, final_norm):
    t = h.shape[0]
    row = pl.BlockSpec((ROW_TILE, D_MODEL), lambda i: (i, 0))
    return pl.pallas_call(
        functools.partial(_ple_kernel, final_norm=final_norm),
        grid=(t // ROW_TILE,),
        in_specs=[row, _resident(g.shape), pl.BlockSpec((ROW_TILE, PLE_DIM), lambda i: (i, 0)),
                  _resident(wpg.shape), _resident(wpe.shape), _resident(gfin.shape)],
        out_specs=row,
        out_shape=jax.ShapeDtypeStruct(h.shape, F32),
        compiler_params=_params(("parallel",)),
        name="ple",
    )(h, g, p, wpg, wpe, gfin)


def _causal_iotas(rows, block):
    r = jnp.bitwise_and(lax.broadcasted_iota(jnp.int32, (rows, block), 0), block - 1)
    c = lax.broadcasted_iota(jnp.int32, (rows, block), 1)
    return r, c


def _softmax_step(s, v, off, c, m_ref, l_ref, acc_ref):
    m_old = m_ref[...]
    m_cur = jnp.max(s, axis=-1, keepdims=True)
    if off is not None:
        m_cur = m_cur + off
    m_new = jnp.maximum(m_old, m_cur)
    alpha = jnp.exp2((m_old - m_new) * c)
    shift = m_new if off is None else m_new - off
    ps = [jnp.exp2((s[:, j * LANES:(j + 1) * LANES] - shift) * c) for j in range(s.shape[1] // LANES)]
    l_ref[...] = alpha * l_ref[...] + functools.reduce(lambda a, b: a + b, ps)
    p = jnp.concatenate([x.astype(BF16) for x in ps], axis=1)
    acc_ref[...] = alpha * acc_ref[...] + _dot(p, v)
    m_ref[...] = m_new


def _softmax_init(m_ref, l_ref, acc_ref):
    m_ref[...] = jnp.full(m_ref.shape, NEG, F32)
    l_ref[...] = jnp.zeros(l_ref.shape, F32)
    acc_ref[...] = jnp.zeros(acc_ref.shape, F32)


def _lane_is_low(shape):
    return lax.broadcasted_iota(jnp.int32, shape, len(shape) - 1) < HALF


def _mla_kernel(q_ref, k_ref, v_ref, o_ref, m_ref, l_ref, acc_ref):
    qi = pl.program_id(2)
    c_exp = (MLA_NOPE + MLA_ROPE) ** -0.5 * LOG2E
    q_a, q_b = q_ref[0, 0], q_ref[0, 1]
    _softmax_init(m_ref, l_ref, acc_ref)

    def scores(kb):
        ks = pl.ds(pl.multiple_of(kb * ATT_BLOCK, ATT_BLOCK), ATT_BLOCK)
        s = jnp.concatenate([_dot_nt(q_a, k_ref[0, 0, ks, :]), _dot_nt(q_b, k_ref[0, 1, ks, :])], axis=0)
        return s, v_ref[0, 0, ks, :]

    def body(kb, carry):
        s, v = scores(kb)
        _softmax_step(s, v, None, c_exp, m_ref, l_ref, acc_ref)
        return carry

    lax.fori_loop(0, qi, body, 0)
    s, v = scores(qi)
    r, c = _causal_iotas(2 * ATT_BLOCK, ATT_BLOCK)
    _softmax_step(jnp.where(c <= r, s, NEG), v, None, c_exp, m_ref, l_ref, acc_ref)

    o = acc_ref[...] / jnp.sum(l_ref[...], axis=-1, keepdims=True)
    o_ref[0] = jnp.where(_lane_is_low((ATT_BLOCK, LANES)), o[:ATT_BLOCK], o[ATT_BLOCK:]).astype(BF16)


def _att_specs(n_q_heads, n_k_heads, seq, block):
    q = pl.BlockSpec((1, n_q_heads, block, LANES), lambda b, g, i: (b, g, i, 0))
    k = pl.BlockSpec((1, n_k_heads, seq, LANES), lambda b, g, i: (b, g, 0, 0))
    v = pl.BlockSpec((1, 1, seq, LANES), lambda b, g, i: (b, g, 0, 0))
    o = pl.BlockSpec((1, block, LANES), lambda b, g, i: (b, i, g))
    return q, k, v, o


def _att_scratch():
    rows = 2 * ATT_BLOCK
    return [pltpu.VMEM((rows, LANES), F32), pltpu.VMEM((rows, LANES), F32), pltpu.VMEM((rows, LANES), F32)]


def _mla(q, k, v):
    b, _, s, _ = q.shape
    groups = MLA_HEADS // 2
    qs, ks, vs, os_ = _att_specs(2, 2, s, ATT_BLOCK)
    return pl.pallas_call(
        _mla_kernel,
        grid=(b, groups, s // ATT_BLOCK),
        in_specs=[qs, ks, vs],
        out_specs=os_,
        out_shape=jax.ShapeDtypeStruct((b, s, groups * LANES), BF16),
        scratch_shapes=_att_scratch(),
        compiler_params=_params(("parallel", "parallel", "arbitrary")),
        name="mla",
    )(q, k, v)


def _diff_kernel(q_ref, k_ref, v_ref, slope_ref, lam_ref, gsub_ref, o_ref, m_ref, l_ref, acc_ref,
                 bias_ref, *, lam_init):
    qi = pl.program_id(2)
    q = q_ref[0, 0]
    low = _lane_is_low(q.shape)
    zero = jnp.zeros_like(q)
    qq = jnp.concatenate([jnp.where(low, q, zero), jnp.where(low, zero, q)], axis=0)
    slope = slope_ref[0][:, :1]
    r, c = _causal_iotas(2 * ATT_BLOCK, ATT_BLOCK)

    @pl.when(qi == 0)
    def _():
        bias_ref[...] = slope * (c - r).astype(F32)

    _softmax_init(m_ref, l_ref, acc_ref)

    def scores(kb):
        ks = pl.ds(pl.multiple_of(kb * ATT_BLOCK, ATT_BLOCK), ATT_BLOCK)
        off = slope * ((kb - qi) * ATT_BLOCK).astype(F32)
        return _dot_nt(qq, k_ref[0, 0, ks, :]) + bias_ref[...], off, v_ref[0, 0, ks, :]

    def body(kb, carry):
        s, off, v = scores(kb)
        _softmax_step(s, v, off, LOG2E, m_ref, l_ref, acc_ref)
        return carry

    lax.fori_loop(0, qi, body, 0)
    s, off, v = scores(qi)
    _softmax_step(jnp.where(c <= r, s, NEG), v, off, LOG2E, m_ref, l_ref, acc_ref)

    lp = lam_ref[...]
    lam = (jnp.exp(jnp.sum(lp[0:1] * lp[1:2], axis=-1, keepdims=True))
           - jnp.exp(jnp.sum(lp[2:3] * lp[3:4], axis=-1, keepdims=True)) + lam_init)
    o = acc_ref[...] / jnp.sum(l_ref[...], axis=-1, keepdims=True)
    o = o[:ATT_BLOCK] - lam * o[ATT_BLOCK:]
    o_ref[0] = (_rms(o, gsub_ref[...]) * (1.0 - lam_init)).astype(BF16)


def _diff(q, k, v, slopes, lam_p, gsub, lam_init):
    b, _, s, _ = q.shape
    qs, ks, vs, os_ = _att_specs(1, 1, s, ATT_BLOCK)
    return pl.pallas_call(
        functools.partial(_diff_kernel, lam_init=lam_init),
        grid=(b, DIFF_HEADS, s // ATT_BLOCK),
        in_specs=[qs, ks, vs, pl.BlockSpec((1, 1, LANES), lambda b_, g, i: (g, 0, 0)),
                  pl.BlockSpec(lam_p.shape, lambda b_, g, i: (0, 0)),
                  pl.BlockSpec(gsub.shape, lambda b_, g, i: (0, 0))],
        out_specs=os_,
        out_shape=jax.ShapeDtypeStruct((b, s, DIFF_HEADS * LANES), BF16),
        scratch_shapes=_att_scratch() + [pltpu.VMEM((2 * ATT_BLOCK, ATT_BLOCK), F32)],
        compiler_params=_params(("parallel", "parallel", "arbitrary")),
        name="diff",
    )(q, k, v, slopes, lam_p, gsub)


def _sb_kernel(q_ref, k_ref, v_ref, o_ref, run_ref, acc_ref):
    qi = pl.program_id(2)
    q = q_ref[0, 0]
    low = _lane_is_low(q.shape)
    zero = jnp.zeros_like(q)
    qq = jnp.concatenate([jnp.where(low, q, zero), jnp.where(low, zero, q)], axis=0)
    r, c = _causal_iotas(2 * SB_BLOCK, SB_BLOCK)
    strict = c < r
    kr = lax.broadcasted_iota(jnp.int32, (SB_BLOCK, SB_BLOCK), 0)
    kc = lax.broadcasted_iota(jnp.int32, (SB_BLOCK, SB_BLOCK), 1)
    later = (kr > kc).astype(BF16)
    run_ref[...] = jnp.zeros(run_ref.shape, F32)
    acc_ref[...] = jnp.zeros(acc_ref.shape, F32)

    def step(kb, diagonal):
        ks = pl.ds(pl.multiple_of(kb * SB_BLOCK, SB_BLOCK), SB_BLOCK)
        s = _dot_nt(qq, k_ref[0, 0, ks, :])
        soft = jnp.log(1.0 + jnp.exp(-jnp.abs(s)))
        log_beta = jnp.minimum(s, 0.0) - soft
        log_1m = log_beta - s
        if diagonal:
            log_1m = jnp.where(strict, log_1m, 0.0)
        hi = log_1m.astype(BF16)
        lo = (log_1m - hi.astype(F32)).astype(BF16)
        suffix = _dot(hi, later) + _dot(lo, later)
        run = run_ref[...]
        w = jnp.exp(log_beta + suffix + run)
        if diagonal:
            w = jnp.where(strict, w, 0.0)
        acc_ref[...] += _dot(w.astype(BF16), v_ref[0, 0, ks, :])
        run = run + jnp.sum(log_1m, axis=-1, keepdims=True)
        run_ref[...] = run
        return jnp.max(run)

    top = step(qi, True)

    def cond(carry):
        kb, top = carry
        return jnp.logical_and(kb >= 0, top > SB_DEAD)

    def body(carry):
        kb, _ = carry
        return kb - 1, step(kb, False)

    lax.while_loop(cond, body, (qi - 1, top))

    o = acc_ref[...]
    o_ref[0] = jnp.where(_lane_is_low((SB_BLOCK, LANES)), o[:SB_BLOCK], o[SB_BLOCK:]).astype(BF16)


def _sb(q, k, v):
    b, groups, s, _ = q.shape
    qs, ks, vs, os_ = _att_specs(1, 1, s, SB_BLOCK)
    rows = 2 * SB_BLOCK
    return pl.pallas_call(
        _sb_kernel,
        grid=(b, groups, s // SB_BLOCK),
        in_specs=[qs, ks, vs],
        out_specs=os_,
        out_shape=jax.ShapeDtypeStruct((b, s, groups * LANES), BF16),
        scratch_shapes=[pltpu.VMEM((rows, 1), F32), pltpu.VMEM((rows, LANES), F32)],
        compiler_params=_params(("parallel", "parallel", "arbitrary")),
        name="sb",
    )(q, k, v)


def _rope_tables(seq):
    half = MLA_ROPE // 2
    freqs = 1.0 / (ROPE_THETA ** (jnp.arange(half, dtype=F32) / half))
    ang = jnp.arange(seq, dtype=F32)[:, None] * freqs[None, :]
    cos, sin = jnp.cos(ang), jnp.sin(ang)
    one = jnp.ones((seq, MLA_NOPE), F32)
    zn = jnp.zeros((seq, MLA_NOPE), F32)
    zh = jnp.zeros((seq, half), F32)
    zt = jnp.zeros((seq, LANES - MLA_NOPE - MLA_ROPE), F32)
    cos_t = jnp.concatenate([one, cos, cos, zt], axis=1)
    sin_a = jnp.concatenate([zn, -sin, zh, zt], axis=1)
    sin_b = jnp.concatenate([zn, zh, sin, zt], axis=1)
    return cos_t, sin_a, sin_b


def _layer_weights(w_in, w_uq, w_ukv):
    pad = LANES - MLA_NOPE - MLA_ROPE
    wc = w_in[:, :MLA_Q_LORA + MLA_KV_LORA]
    wkr = jnp.pad(w_in[:, MLA_Q_LORA + MLA_KV_LORA:MLA_IN], ((0, 0), (MLA_NOPE, pad)))
    wd = w_in[:, MLA_IN:MLA_IN + DIFF_IN]
    ws = w_in[:, MLA_IN + DIFF_IN:]
    wuq = jnp.pad(w_uq.reshape(MLA_Q_LORA, MLA_HEADS, MLA_NOPE + MLA_ROPE),
                  ((0, 0), (0, 0), (0, pad))).reshape(MLA_Q_LORA, MLA_HEADS * LANES)
    kv = w_ukv.reshape(MLA_KV_LORA, MLA_HEADS, MLA_NOPE + MLA_V)
    wuk = jnp.pad(kv[:, :, :MLA_NOPE], ((0, 0), (0, 0), (0, LANES - MLA_NOPE))).reshape(
        MLA_KV_LORA, MLA_HEADS * LANES)
    wuv = kv[:, :, MLA_NOPE:].reshape(MLA_KV_LORA, MLA_HEADS * MLA_V)
    return [w.astype(BF16) for w in (wc, wkr, wd, ws, wuq, wuk, wuv)]


def kernel(x, p, g_ffn1, w1_a, w3_a, w2_a, g_mix, w_in, g_cq, g_ckv, w_uq, w_ukv, lambda_q1, lambda_k1, lambda_q2, lambda_k2, g_subln, w_o_mla, w_o_diff, w_o_sb, w_branch_gate, w_out, g_ffn2, w1_b, w3_b, w2_b, g_ple, w_ple_gate, w_ple_proj, g_final):
    b, s, d = x.shape
    t = b * s
    assert d == D_MODEL and t % ROW_TILE == 0 and s % PROJ_TILE == 0 and s % ATT_BLOCK == 0
    cos_t, sin_a, sin_b = _rope_tables(s)
    slopes = np.exp2(-8.0 * np.arange(1, DIFF_HEADS + 1, dtype=np.float32) / DIFF_HEADS)
    slopes = jnp.asarray(np.broadcast_to(slopes[:, None, None], (DIFF_HEADS, 1, LANES)), F32)
    row = lambda v: v.reshape(1, -1)
    bf = lambda w: w.astype(BF16)

    h = x.reshape(t, d)
    for i in range(DEPTH):
        lam_init = 0.8 - 0.6 * math.exp(-0.3 * i)
        h = _ffn(h, row(g_ffn1[i]), bf(w1_a[i]), bf(w3_a[i]), bf(w2_a[i]))
        wc, wkr, wd, ws, wuq, wuk, wuv = _layer_weights(w_in[i], w_uq[i], w_ukv[i])
        qm, km, vm, qd, kd, vd, qs, ks, vs = _proj(
            h.reshape(b, s, d), row(g_mix[i]), wc, wkr, wd, ws, row(g_cq[i]), row(g_ckv[i]),
            wuq, wuk, wuv, cos_t, sin_a, sin_b)
        y_mla = _mla(qm, km, vm).reshape(t, -1)
        lam_p = jnp.stack([lambda_q1[i], lambda_k1[i], lambda_q2[i], lambda_k2[i]])
        y_diff = _diff(qd, kd, vd, slopes, lam_p, row(g_subln[i]), lam_init).reshape(t, -1)
        y_sb = _sb(qs, ks, vs).reshape(t, -1)
        h = _merge(h, row(g_mix[i]), y_mla, y_diff, y_sb, bf(w_branch_gate[i]), bf(w_o_mla[i]),
                   bf(w_o_diff[i]), bf(w_o_sb[i]), bf(w_out[i]))
        h = _ffn(h, row(g_ffn2[i]), bf(w1_b[i]), bf(w3_b[i]), bf(w2_b[i]))
        h = _ple(h, row(g_ple[i]), p[i].reshape(t, PLE_DIM), bf(w_ple_gate[i]), bf(w_ple_proj[i]),
                 row(g_final), final_norm=(i == DEPTH - 1))
    return h.reshape(b, s, d)
```

```python
import functools
import math

import jax
import jax.numpy as jnp
import numpy as np
from jax import lax
from jax.experimental import pallas as pl
from jax.experimental.pallas import tpu as pltpu

D_MODEL = 1024
DEPTH = 2
PLE_DIM = 256
EPS = 1e-6
ROPE_THETA = 10000.0

MLA_HEADS = 8
MLA_Q_LORA = 384
MLA_KV_LORA = 256
MLA_NOPE = 64
MLA_ROPE = 32
MLA_V = 64
MLA_IN = MLA_Q_LORA + MLA_KV_LORA + MLA_ROPE

DIFF_HEADS = 4
DIFF_QK = 64
DIFF_V = 2 * DIFF_QK
DIFF_QK_W = DIFF_HEADS * 2 * DIFF_QK
DIFF_IN = 2 * DIFF_QK_W + DIFF_HEADS * DIFF_V

SB_HEADS = 8
SB_DIM = 64
SB_W = SB_HEADS * SB_DIM
SB_IN = 3 * SB_W

N_BRANCH = 3
D_FF = 2816

LANES = 128
HALF = LANES // 2
VMEM_LIMIT = 56 * 1024 * 1024
NEG = -1e30
SB_DEAD = -104.0

ROW_TILE = 512
PROJ_TILE = 256
ATT_BLOCK = 512
SB_BLOCK = 256
LOG2E = math.log2(math.e)
MLA_Q_SCALE = (MLA_NOPE + MLA_ROPE) ** -0.5 * LOG2E
DIFF_Q_SCALE = DIFF_QK ** -0.5 * LOG2E

BF16 = jnp.bfloat16
F32 = jnp.float32


def _dot(a, b):
    return jnp.dot(a, b, preferred_element_type=F32)


def _dot_nt(a, b):
    return lax.dot_general(a, b, (((1,), (1,)), ((), ())), preferred_element_type=F32)


def _rms(x, g):
    ms = jnp.mean(x * x, axis=-1, keepdims=True)
    return x * lax.rsqrt(ms + EPS) * g


def _sigmoid(x):
    return 1.0 / (1.0 + jnp.exp(-x))


def _resident(shape):
    nd = len(shape)
    return pl.BlockSpec(shape, lambda *_: (0,) * nd, pipeline_mode=pl.Buffered(1))


def _params(sem):
    return pltpu.CompilerParams(dimension_semantics=sem, vmem_limit_bytes=VMEM_LIMIT)


def _ffn_kernel(h_ref, g_ref, w1_ref, w3_ref, w2_ref, o_ref):
    x = h_ref[...]
    n = _rms(x, g_ref[...]).astype(BF16)
    a = _dot(n, w1_ref[...])
    b = _dot(n, w3_ref[...])
    act = (a * _sigmoid(a) * b).astype(BF16)
    o_ref[...] = x + 0.5 * _dot(act, w2_ref[...])


def _ffn(h, g, w1, w3, w2):
    t = h.shape[0]
    row = pl.BlockSpec((ROW_TILE, D_MODEL), lambda i: (i, 0))
    return pl.pallas_call(
        _ffn_kernel,
        grid=(t // ROW_TILE,),
        in_specs=[row, _resident((1, D_MODEL)), _resident(w1.shape), _resident(w3.shape),
                  _resident(w2.shape)],
        out_specs=row,
        out_shape=jax.ShapeDtypeStruct(h.shape, F32),
        compiler_params=_params(("parallel",)),
        name="ffn",
    )(h, g, w1, w3, w2)


def _rope(x, cos, sin_a, sin_b):
    return (x * cos + pltpu.roll(x, LANES - MLA_ROPE // 2, 1) * sin_a
            + pltpu.roll(x, MLA_ROPE // 2, 1) * sin_b)


def _proj_kernel(h_ref, g_ref, wc_ref, wkr_ref, wd_ref, ws_ref, gcq_ref, gckv_ref, wuq_ref,
                 wuk_ref, wuv_ref, cos_ref, sa_ref, sb_ref,
                 qm_ref, km_ref, vm_ref, qd_ref, kd_ref, vd_ref, qs_ref, ks_ref, vs_ref):
    u = _rms(h_ref[0], g_ref[...]).astype(BF16)
    cos, sin_a, sin_b = cos_ref[...], sa_ref[...], sb_ref[...]

    c = _dot(u, wc_ref[...])
    cq = _rms(c[:, :MLA_Q_LORA], gcq_ref[...]).astype(BF16)
    ckv = _rms(c[:, MLA_Q_LORA:], gckv_ref[...]).astype(BF16)
    q = _dot(cq, wuq_ref[...])
    kn = _dot(ckv, wuk_ref[...])
    v = _dot(ckv, wuv_ref[...])
    kr = _rope(_dot(u, wkr_ref[...]), cos, sin_a, sin_b)
    for hd in range(MLA_HEADS):
        sl = slice(hd * LANES, (hd + 1) * LANES)
        qm_ref[0, hd] = (_rope(q[:, sl], cos, sin_a, sin_b) * MLA_Q_SCALE).astype(BF16)
        km_ref[0, hd] = (kn[:, sl] + kr).astype(BF16)
    for p in range(MLA_HEADS // 2):
        vm_ref[0, p] = v[:, p * LANES:(p + 1) * LANES].astype(BF16)

    d = _dot(u, wd_ref[...])
    for hd in range(DIFF_HEADS):
        sl = slice(hd * LANES, (hd + 1) * LANES)
        qd_ref[0, hd] = (d[:, sl] * DIFF_Q_SCALE).astype(BF16)
        kd_ref[0, hd] = d[:, DIFF_QK_W + hd * LANES:DIFF_QK_W + (hd + 1) * LANES].astype(BF16)
        vd_ref[0, hd] = d[:, 2 * DIFF_QK_W + hd * LANES:2 * DIFF_QK_W + (hd + 1) * LANES].astype(BF16)

    s = _dot(u, ws_ref[...])
    for p in range(SB_HEADS // 2):
        sl = slice(p * LANES, (p + 1) * LANES)
        qs_ref[0, p] = (s[:, sl] * SB_DIM ** -0.5).astype(BF16)
        ks_ref[0, p] = s[:, SB_W + p * LANES:SB_W + (p + 1) * LANES].astype(BF16)
        vs_ref[0, p] = s[:, 2 * SB_W + p * LANES:2 * SB_W + (p + 1) * LANES].astype(BF16)


def _proj(h3, g, wc, wkr, wd, ws, gcq, gckv, wuq, wuk, wuv, cos, sin_a, sin_b):
    b, s, _ = h3.shape
    tab = pl.BlockSpec((PROJ_TILE, LANES), lambda bi, si: (si, 0))

    def heads(n):
        return pl.BlockSpec((1, n, PROJ_TILE, LANES), lambda bi, si: (bi, 0, si, 0))

    def out(n):
        return jax.ShapeDtypeStruct((b, n, s, LANES), BF16)

    n_out = [MLA_HEADS, MLA_HEADS, MLA_HEADS // 2, DIFF_HEADS, DIFF_HEADS, DIFF_HEADS,
             SB_HEADS // 2, SB_HEADS // 2, SB_HEADS // 2]
    return pl.pallas_call(
        _proj_kernel,
        grid=(b, s // PROJ_TILE),
        in_specs=[pl.BlockSpec((1, PROJ_TILE, D_MODEL), lambda bi, si: (bi, si, 0)),
                  _resident(g.shape), _resident(wc.shape), _resident(wkr.shape),
                  _resident(wd.shape), _resident(ws.shape), _resident(gcq.shape),
                  _resident(gckv.shape), _resident(wuq.shape), _resident(wuk.shape),
                  _resident(wuv.shape), tab, tab, tab],
        out_specs=[heads(n) for n in n_out],
        out_shape=[out(n) for n in n_out],
        compiler_params=_params(("parallel", "parallel")),
        name="proj",
    )(h3, g, wc, wkr, wd, ws, gcq, gckv, wuq, wuk, wuv, cos, sin_a, sin_b)


def _merge_kernel(h_ref, g_ref, ya_ref, yb_ref, yc_ref, wg_ref, woa_ref, wob_ref, woc_ref,
                  wout_ref, o_ref):
    x = h_ref[...]
    u = _rms(x, g_ref[...]).astype(BF16)
    merged = None
    for i, (y_ref, wo_ref) in enumerate(((ya_ref, woa_ref), (yb_ref, wob_ref), (yc_ref, woc_ref))):
        gate = _sigmoid(_dot(u, wg_ref[:, i * D_MODEL:(i + 1) * D_MODEL]))
        term = gate * _dot(y_ref[...], wo_ref[...])
        merged = term if merged is None else merged + term
    o_ref[...] = x + _dot(merged.astype(BF16), wout_ref[...])


def _merge(h, g, ya, yb, yc, wg, woa, wob, woc, wout):
    t = h.shape[0]
    row = pl.BlockSpec((ROW_TILE, D_MODEL), lambda i: (i, 0))
    yrow = pl.BlockSpec((ROW_TILE, ya.shape[1]), lambda i: (i, 0))
    return pl.pallas_call(
        _merge_kernel,
        grid=(t // ROW_TILE,),
        in_specs=[row, _resident(g.shape), yrow, yrow, yrow, _resident(wg.shape),
                  _resident(woa.shape), _resident(wob.shape), _resident(woc.shape),
                  _resident(wout.shape)],
        out_specs=row,
        out_shape=jax.ShapeDtypeStruct(h.shape, F32),
        compiler_params=_params(("parallel",)),
        name="merge",
    )(h, g, ya, yb, yc, wg, woa, wob, woc, wout)


def _ple_kernel(h_ref, g_ref, p_ref, wpg_ref, wpe_ref, gfin_ref, o_ref, *, final_norm):
    x = h_ref[...]
    gate = _sigmoid(_dot(_rms(x, g_ref[...]).astype(BF16), wpg_ref[...]))
    y = x + gate * _dot(p_ref[...].astype(BF16), wpe_ref[...])
    if final_norm:
        y = _rms(y, gfin_ref[...])
    o_ref[...] = y


def _ple(h, g, p, wpg, wpe, gfin, final_norm):
    t = h.shape[0]
    row = pl.BlockSpec((ROW_TILE, D_MODEL), lambda i: (i, 0))
    return pl.pallas_call(
        functools.partial(_ple_kernel, final_norm=final_norm),
        grid=(t // ROW_TILE,),
        in_specs=[row, _resident(g.shape), pl.BlockSpec((ROW_TILE, PLE_DIM), lambda i: (i, 0)),
                  _resident(wpg.shape), _resident(wpe.shape), _resident(gfin.shape)],
        out_specs=row,
        out_shape=jax.ShapeDtypeStruct(h.shape, F32),
        compiler_params=_params(("parallel",)),
        name="ple",
    )(h, g, p, wpg, wpe, gfin)


def _causal_iotas(rows, block):
    r = jnp.bitwise_and(lax.broadcasted_iota(jnp.int32, (rows, block), 0), block - 1)
    c = lax.broadcasted_iota(jnp.int32, (rows, block), 1)
    return r, c


def _softmax_step(s, v, decay, m_ref, l_ref, acc_ref):
    m_old = m_ref[...]
    if decay is not None:
        m_old = m_old - decay
    m_new = jnp.maximum(m_old, jnp.max(s, axis=-1, keepdims=True))
    alpha = jnp.exp2(m_old - m_new)
    ps = [jnp.exp2(s[:, j * LANES:(j + 1) * LANES] - m_new) for j in range(s.shape[1] // LANES)]
    l_ref[...] = alpha * l_ref[...] + functools.reduce(lambda a, b: a + b, ps)
    p = jnp.concatenate([x.astype(BF16) for x in ps], axis=1)
    acc_ref[...] = alpha * acc_ref[...] + _dot(p, v)
    m_ref[...] = m_new


def _softmax_walk(qi, scores, values, mask, decay, s_ref, m_ref, l_ref, acc_ref):
    _softmax_init(m_ref, l_ref, acc_ref)
    s_ref[...] = scores(0)

    def body(kb, carry):
        s = s_ref[...]
        s_next = scores(kb + 1)
        _softmax_step(s, values(kb), decay, m_ref, l_ref, acc_ref)
        s_ref[...] = s_next
        return carry

    lax.fori_loop(0, qi, body, 0)
    _softmax_step(jnp.where(mask, s_ref[...], NEG), values(qi), decay, m_ref, l_ref, acc_ref)
    return acc_ref[...] / jnp.sum(l_ref[...], axis=-1, keepdims=True)


def _softmax_init(m_ref, l_ref, acc_ref):
    m_ref[...] = jnp.full(m_ref.shape, NEG, F32)
    l_ref[...] = jnp.zeros(l_ref.shape, F32)
    acc_ref[...] = jnp.zeros(acc_ref.shape, F32)


def _lane_is_low(shape):
    return lax.broadcasted_iota(jnp.int32, shape, len(shape) - 1) < HALF


def _mla_kernel(q_ref, k_ref, v_ref, o_ref, m_ref, l_ref, acc_ref, s_ref):
    qi = pl.program_id(2)
    q_a, q_b = q_ref[0, 0], q_ref[0, 1]

    def block(kb):
        return pl.ds(pl.multiple_of(kb * ATT_BLOCK, ATT_BLOCK), ATT_BLOCK)

    def scores(kb):
        ks = block(kb)
        return jnp.concatenate([_dot_nt(q_a, k_ref[0, 0, ks, :]), _dot_nt(q_b, k_ref[0, 1, ks, :])], axis=0)

    def values(kb):
        return v_ref[0, 0, block(kb), :]

    r, c = _causal_iotas(2 * ATT_BLOCK, ATT_BLOCK)
    o = _softmax_walk(qi, scores, values, c <= r, None, s_ref, m_ref, l_ref, acc_ref)
    o_ref[0] = jnp.where(_lane_is_low((ATT_BLOCK, LANES)), o[:ATT_BLOCK], o[ATT_BLOCK:]).astype(BF16)


def _att_specs(n_q_heads, n_k_heads, seq, block):
    q = pl.BlockSpec((1, n_q_heads, block, LANES), lambda b, g, i: (b, g, i, 0))
    k = pl.BlockSpec((1, n_k_heads, seq, LANES), lambda b, g, i: (b, g, 0, 0))
    v = pl.BlockSpec((1, 1, seq, LANES), lambda b, g, i: (b, g, 0, 0))
    o = pl.BlockSpec((1, block, LANES), lambda b, g, i: (b, i, g))
    return q, k, v, o


def _att_scratch():
    rows = 2 * ATT_BLOCK
    return [pltpu.VMEM((rows, LANES), F32), pltpu.VMEM((rows, LANES), F32), pltpu.VMEM((rows, LANES), F32),
            pltpu.VMEM((rows, ATT_BLOCK), F32)]


def _mla(q, k, v):
    b, _, s, _ = q.shape
    groups = MLA_HEADS // 2
    qs, ks, vs, os_ = _att_specs(2, 2, s, ATT_BLOCK)
    return pl.pallas_call(
        _mla_kernel,
        grid=(b, groups, s // ATT_BLOCK),
        in_specs=[qs, ks, vs],
        out_specs=os_,
        out_shape=jax.ShapeDtypeStruct((b, s, groups * LANES), BF16),
        scratch_shapes=_att_scratch(),
        compiler_params=_params(("parallel", "parallel", "arbitrary")),
        name="mla",
    )(q, k, v)


def _diff_kernel(slope_ref, q_ref, k_ref, v_ref, lam_ref, gsub_ref, o_ref, m_ref, l_ref, acc_ref,
                 s_ref, bias_ref, *, lam_init):
    qi = pl.program_id(2)
    q = q_ref[0, 0]
    low = _lane_is_low(q.shape)
    zero = jnp.zeros_like(q)
    qq = jnp.concatenate([jnp.where(low, q, zero), jnp.where(low, zero, q)], axis=0)
    slope = slope_ref[pl.program_id(1)] * LOG2E
    r, c = _causal_iotas(2 * ATT_BLOCK, ATT_BLOCK)

    @pl.when(qi == 0)
    def _():
        bias_ref[...] = slope * (c - r).astype(F32)

    def block(kb):
        return pl.ds(pl.multiple_of(kb * ATT_BLOCK, ATT_BLOCK), ATT_BLOCK)

    def scores(kb):
        return _dot_nt(qq, k_ref[0, 0, block(kb), :]) + bias_ref[...]

    def values(kb):
        return v_ref[0, 0, block(kb), :]

    o = _softmax_walk(qi, scores, values, c <= r, slope * ATT_BLOCK, s_ref, m_ref, l_ref, acc_ref)

    lp = lam_ref[...]
    lam = (jnp.exp(jnp.sum(lp[0:1] * lp[1:2], axis=-1, keepdims=True))
           - jnp.exp(jnp.sum(lp[2:3] * lp[3:4], axis=-1, keepdims=True)) + lam_init)
    o = o[:ATT_BLOCK] - lam * o[ATT_BLOCK:]
    o_ref[0] = (_rms(o, gsub_ref[...]) * (1.0 - lam_init)).astype(BF16)


def _diff(q, k, v, slopes, lam_p, gsub, lam_init):
    b, _, s, _ = q.shape
    qs, ks, vs, os_ = _att_specs(1, 1, s, ATT_BLOCK)
    return pl.pallas_call(
        functools.partial(_diff_kernel, lam_init=lam_init),
        grid=(b, DIFF_HEADS, s // ATT_BLOCK),
        in_specs=[pl.BlockSpec(memory_space=pltpu.SMEM), qs, ks, vs,
                  pl.BlockSpec(lam_p.shape, lambda b_, g, i: (0, 0)),
                  pl.BlockSpec(gsub.shape, lambda b_, g, i: (0, 0))],
        out_specs=os_,
        out_shape=jax.ShapeDtypeStruct((b, s, DIFF_HEADS * LANES), BF16),
        scratch_shapes=_att_scratch() + [pltpu.VMEM((2 * ATT_BLOCK, ATT_BLOCK), F32)],
        compiler_params=_params(("parallel", "parallel", "arbitrary")),
        name="diff",
    )(slopes, q, k, v, lam_p, gsub)


def _sb_kernel(q_ref, k_ref, v_ref, o_ref, run_ref, acc_ref):
    qi = pl.program_id(2)
    q = q_ref[0, 0]
    low = _lane_is_low(q.shape)
    zero = jnp.zeros_like(q)
    qq = jnp.concatenate([jnp.where(low, q, zero), jnp.where(low, zero, q)], axis=0)
    r, c = _causal_iotas(2 * SB_BLOCK, SB_BLOCK)
    strict = c < r
    kr = lax.broadcasted_iota(jnp.int32, (SB_BLOCK, SB_BLOCK), 0)
    kc = lax.broadcasted_iota(jnp.int32, (SB_BLOCK, SB_BLOCK), 1)
    later = (kr > kc).astype(BF16)
    run_ref[...] = jnp.zeros(run_ref.shape, F32)
    acc_ref[...] = jnp.zeros(acc_ref.shape, F32)

    def step(kb, diagonal):
        ks = pl.ds(pl.multiple_of(kb * SB_BLOCK, SB_BLOCK), SB_BLOCK)
        s = _dot_nt(qq, k_ref[0, 0, ks, :])
        soft = jnp.log(1.0 + jnp.exp(-jnp.abs(s)))
        log_beta = jnp.minimum(s, 0.0) - soft
        log_1m = log_beta - s
        if diagonal:
            log_1m = jnp.where(strict, log_1m, 0.0)
        hi = log_1m.astype(BF16)
        lo = (log_1m - hi.astype(F32)).astype(BF16)
        suffix = _dot(hi, later) + _dot(lo, later)
        run = run_ref[...]
        w = jnp.exp(log_beta + suffix + jnp.concatenate([run] * (SB_BLOCK // LANES), axis=1))
        if diagonal:
            w = jnp.where(strict, w, 0.0)
        acc_ref[...] += _dot(w.astype(BF16), v_ref[0, 0, ks, :])
        run = run + jnp.sum(log_1m, axis=-1, keepdims=True)
        run_ref[...] = run
        return jnp.max(run)

    top = step(qi, True)

    def cond(carry):
        kb, top = carry
        return jnp.logical_and(kb >= 0, top > SB_DEAD)

    def body(carry):
        kb, _ = carry
        return kb - 1, step(kb, False)

    lax.while_loop(cond, body, (qi - 1, top))

    o = acc_ref[...]
    o_ref[0] = jnp.where(_lane_is_low((SB_BLOCK, LANES)), o[:SB_BLOCK], o[SB_BLOCK:]).astype(BF16)


def _sb(q, k, v):
    b, groups, s, _ = q.shape
    qs, ks, vs, os_ = _att_specs(1, 1, s, SB_BLOCK)
    rows = 2 * SB_BLOCK
    return pl.pallas_call(
        _sb_kernel,
        grid=(b, groups, s // SB_BLOCK),
        in_specs=[qs, ks, vs],
        out_specs=os_,
        out_shape=jax.ShapeDtypeStruct((b, s, groups * LANES), BF16),
        scratch_shapes=[pltpu.VMEM((rows, LANES), F32), pltpu.VMEM((rows, LANES), F32)],
        compiler_params=_params(("parallel", "parallel", "arbitrary")),
        name="sb",
    )(q, k, v)


def _rope_tables(seq):
    half = MLA_ROPE // 2
    freqs = 1.0 / (ROPE_THETA ** (jnp.arange(half, dtype=F32) / half))
    ang = jnp.arange(seq, dtype=F32)[:, None] * freqs[None, :]
    cos, sin = jnp.cos(ang), jnp.sin(ang)
    one = jnp.ones((seq, MLA_NOPE), F32)
    zn = jnp.zeros((seq, MLA_NOPE), F32)
    zh = jnp.zeros((seq, half), F32)
    zt = jnp.zeros((seq, LANES - MLA_NOPE - MLA_ROPE), F32)
    cos_t = jnp.concatenate([one, cos, cos, zt], axis=1)
    sin_a = jnp.concatenate([zn, -sin, zh, zt], axis=1)
    sin_b = jnp.concatenate([zn, zh, sin, zt], axis=1)
    return cos_t, sin_a, sin_b


def _layer_weights(w_in, w_uq, w_ukv):
    pad = LANES - MLA_NOPE - MLA_ROPE
    wc = w_in[:, :MLA_Q_LORA + MLA_KV_LORA]
    wkr = jnp.pad(w_in[:, MLA_Q_LORA + MLA_KV_LORA:MLA_IN], ((0, 0), (MLA_NOPE, pad)))
    wd = w_in[:, MLA_IN:MLA_IN + DIFF_IN]
    ws = w_in[:, MLA_IN + DIFF_IN:]
    wuq = jnp.pad(w_uq.reshape(MLA_Q_LORA, MLA_HEADS, MLA_NOPE + MLA_ROPE),
                  ((0, 0), (0, 0), (0, pad))).reshape(MLA_Q_LORA, MLA_HEADS * LANES)
    kv = w_ukv.reshape(MLA_KV_LORA, MLA_HEADS, MLA_NOPE + MLA_V)
    wuk = jnp.pad(kv[:, :, :MLA_NOPE], ((0, 0), (0, 0), (0, LANES - MLA_NOPE))).reshape(
        MLA_KV_LORA, MLA_HEADS * LANES)
    wuv = kv[:, :, MLA_NOPE:].reshape(MLA_KV_LORA, MLA_HEADS * MLA_V)
    return [w.astype(BF16) for w in (wc, wkr, wd, ws, wuq, wuk, wuv)]


def kernel(x, p, g_ffn1, w1_a, w3_a, w2_a, g_mix, w_in, g_cq, g_ckv, w_uq, w_ukv, lambda_q1, lambda_k1, lambda_q2, lambda_k2, g_subln, w_o_mla, w_o_diff, w_o_sb, w_branch_gate, w_out, g_ffn2, w1_b, w3_b, w2_b, g_ple, w_ple_gate, w_ple_proj, g_final):
    b, s, d = x.shape
    t = b * s
    assert d == D_MODEL and t % ROW_TILE == 0 and s % PROJ_TILE == 0 and s % ATT_BLOCK == 0
    cos_t, sin_a, sin_b = _rope_tables(s)
    slopes = jnp.asarray(np.exp2(-8.0 * np.arange(1, DIFF_HEADS + 1, dtype=np.float32) / DIFF_HEADS), F32)
    row = lambda v: v.reshape(1, -1)
    bf = lambda w: w.astype(BF16)

    h = x.reshape(t, d)
    for i in range(DEPTH):
        lam_init = 0.8 - 0.6 * math.exp(-0.3 * i)
        h = _ffn(h, row(g_ffn1[i]), bf(w1_a[i]), bf(w3_a[i]), bf(w2_a[i]))
        wc, wkr, wd, ws, wuq, wuk, wuv = _layer_weights(w_in[i], w_uq[i], w_ukv[i])
        qm, km, vm, qd, kd, vd, qs, ks, vs = _proj(
            h.reshape(b, s, d), row(g_mix[i]), wc, wkr, wd, ws, row(g_cq[i]), row(g_ckv[i]),
            wuq, wuk, wuv, cos_t, sin_a, sin_b)
        y_mla = _mla(qm, km, vm).reshape(t, -1)
        lam_p = jnp.stack([lambda_q1[i], lambda_k1[i], lambda_q2[i], lambda_k2[i]])
        y_diff = _diff(qd, kd, vd, slopes, lam_p, row(g_subln[i]), lam_init).reshape(t, -1)
        y_sb = _sb(qs, ks, vs).reshape(t, -1)
        h = _merge(h, row(g_mix[i]), y_mla, y_diff, y_sb, bf(w_branch_gate[i]), bf(w_o_mla[i]),
                   bf(w_o_diff[i]), bf(w_o_sb[i]), bf(w_out[i]))
        h = _ffn(h, row(g_ffn2[i]), bf(w1_b[i]), bf(w3_b[i]), bf(w2_b[i]))
        h = _ple(h, row(g_ple[i]), p[i].reshape(t, PLE_DIM), bf(w_ple_gate[i]), bf(w_ple_proj[i]),
                 row(g_final), final_norm=(i == DEPTH - 1))
    return h.reshape(b, s, d)
```

```python
import functools
import math

import jax
import jax.numpy as jnp
import numpy as np
from jax import lax
from jax.experimental import pallas as pl
from jax.experimental.pallas import tpu as pltpu

D_MODEL = 1024
DEPTH = 2
PLE_DIM = 256
EPS = 1e-6
ROPE_THETA = 10000.0

MLA_HEADS = 8
MLA_Q_LORA = 384
MLA_KV_LORA = 256
MLA_NOPE = 64
MLA_ROPE = 32
MLA_V = 64
MLA_IN = MLA_Q_LORA + MLA_KV_LORA + MLA_ROPE

DIFF_HEADS = 4
DIFF_QK = 64
DIFF_V = 2 * DIFF_QK
DIFF_QK_W = DIFF_HEADS * 2 * DIFF_QK
DIFF_IN = 2 * DIFF_QK_W + DIFF_HEADS * DIFF_V

SB_HEADS = 8
SB_DIM = 64
SB_W = SB_HEADS * SB_DIM
SB_IN = 3 * SB_W

N_BRANCH = 3
D_FF = 2816

LANES = 128
HALF = LANES // 2
VMEM_LIMIT = 56 * 1024 * 1024
NEG = -1e30
SB_DEAD = -104.0

ROW_TILE = 512
PROJ_TILE = 256
ATT_BLOCK = 512
SB_BLOCK = 256
SB_PAIRS = 4
LOG2E = math.log2(math.e)
MLA_Q_SCALE = (MLA_NOPE + MLA_ROPE) ** -0.5 * LOG2E
DIFF_Q_SCALE = DIFF_QK ** -0.5 * LOG2E

BF16 = jnp.bfloat16
F32 = jnp.float32


def _dot(a, b):
    return jnp.dot(a, b, preferred_element_type=F32)


def _dot_nt(a, b):
    return lax.dot_general(a, b, (((1,), (1,)), ((), ())), preferred_element_type=F32)


def _rms(x, g):
    ms = jnp.mean(x * x, axis=-1, keepdims=True)
    return x * lax.rsqrt(ms + EPS) * g


def _sigmoid(x):
    return 1.0 / (1.0 + jnp.exp(-x))


def _resident(shape):
    nd = len(shape)
    return pl.BlockSpec(shape, lambda *_: (0,) * nd, pipeline_mode=pl.Buffered(1))


def _params(sem):
    return pltpu.CompilerParams(dimension_semantics=sem, vmem_limit_bytes=VMEM_LIMIT)


def _ffn_kernel(h_ref, g_ref, w1_ref, w3_ref, w2_ref, o_ref):
    x = h_ref[...]
    n = _rms(x, g_ref[...]).astype(BF16)
    a = _dot(n, w1_ref[...])
    b = _dot(n, w3_ref[...])
    act = (a * _sigmoid(a) * b).astype(BF16)
    o_ref[...] = x + 0.5 * _dot(act, w2_ref[...])


def _ffn(h, g, w1, w3, w2):
    t = h.shape[0]
    row = pl.BlockSpec((ROW_TILE, D_MODEL), lambda i: (i, 0))
    return pl.pallas_call(
        _ffn_kernel,
        grid=(t // ROW_TILE,),
        in_specs=[row, _resident((1, D_MODEL)), _resident(w1.shape), _resident(w3.shape),
                  _resident(w2.shape)],
        out_specs=row,
        out_shape=jax.ShapeDtypeStruct(h.shape, F32),
        compiler_params=_params(("parallel",)),
        name="ffn",
    )(h, g, w1, w3, w2)


def _rope(x, cos, sin_a, sin_b):
    return (x * cos + pltpu.roll(x, LANES - MLA_ROPE // 2, 1) * sin_a
            + pltpu.roll(x, MLA_ROPE // 2, 1) * sin_b)


def _proj_kernel(h_ref, g_ref, wc_ref, wkr_ref, wd_ref, ws_ref, gcq_ref, gckv_ref, wuq_ref,
                 wuk_ref, wuv_ref, cos_ref, sa_ref, sb_ref,
                 qm_ref, km_ref, vm_ref, qd_ref, kd_ref, vd_ref, qs_ref, ks_ref, vs_ref):
    u = _rms(h_ref[0], g_ref[...]).astype(BF16)
    cos, sin_a, sin_b = cos_ref[...], sa_ref[...], sb_ref[...]

    c = _dot(u, wc_ref[...])
    cq = _rms(c[:, :MLA_Q_LORA], gcq_ref[...]).astype(BF16)
    ckv = _rms(c[:, MLA_Q_LORA:], gckv_ref[...]).astype(BF16)
    q = _dot(cq, wuq_ref[...])
    kn = _dot(ckv, wuk_ref[...])
    v = _dot(ckv, wuv_ref[...])
    kr = _rope(_dot(u, wkr_ref[...]), cos, sin_a, sin_b)
    for hd in range(MLA_HEADS):
        sl = slice(hd * LANES, (hd + 1) * LANES)
        qm_ref[0, hd] = (_rope(q[:, sl], cos, sin_a, sin_b) * MLA_Q_SCALE).astype(BF16)
        km_ref[0, hd] = (kn[:, sl] + kr).astype(BF16)
    for p in range(MLA_HEADS // 2):
        vm_ref[0, p] = v[:, p * LANES:(p + 1) * LANES].astype(BF16)

    d = _dot(u, wd_ref[...])
    for hd in range(DIFF_HEADS):
        sl = slice(hd * LANES, (hd + 1) * LANES)
        qd_ref[0, hd] = (d[:, sl] * DIFF_Q_SCALE).astype(BF16)
        kd_ref[0, hd] = d[:, DIFF_QK_W + hd * LANES:DIFF_QK_W + (hd + 1) * LANES].astype(BF16)
        vd_ref[0, hd] = d[:, 2 * DIFF_QK_W + hd * LANES:2 * DIFF_QK_W + (hd + 1) * LANES].astype(BF16)

    s = _dot(u, ws_ref[...])
    for p in range(SB_HEADS // 2):
        sl = slice(p * LANES, (p + 1) * LANES)
        qs_ref[0, p] = (s[:, sl] * SB_DIM ** -0.5).astype(BF16)
        ks_ref[0, p] = s[:, SB_W + p * LANES:SB_W + (p + 1) * LANES].astype(BF16)
        vs_ref[0, p] = s[:, 2 * SB_W + p * LANES:2 * SB_W + (p + 1) * LANES].astype(BF16)


def _proj(h3, g, wc, wkr, wd, ws, gcq, gckv, wuq, wuk, wuv, cos, sin_a, sin_b):
    b, s, _ = h3.shape
    tab = pl.BlockSpec((PROJ_TILE, LANES), lambda bi, si: (si, 0))

    def heads(n):
        return pl.BlockSpec((1, n, PROJ_TILE, LANES), lambda bi, si: (bi, 0, si, 0))

    def out(n):
        return jax.ShapeDtypeStruct((b, n, s, LANES), BF16)

    n_out = [MLA_HEADS, MLA_HEADS, MLA_HEADS // 2, DIFF_HEADS, DIFF_HEADS, DIFF_HEADS,
             SB_HEADS // 2, SB_HEADS // 2, SB_HEADS // 2]
    return pl.pallas_call(
        _proj_kernel,
        grid=(b, s // PROJ_TILE),
        in_specs=[pl.BlockSpec((1, PROJ_TILE, D_MODEL), lambda bi, si: (bi, si, 0)),
                  _resident(g.shape), _resident(wc.shape), _resident(wkr.shape),
                  _resident(wd.shape), _resident(ws.shape), _resident(gcq.shape),
                  _resident(gckv.shape), _resident(wuq.shape), _resident(wuk.shape),
                  _resident(wuv.shape), tab, tab, tab],
        out_specs=[heads(n) for n in n_out],
        out_shape=[out(n) for n in n_out],
        compiler_params=_params(("parallel", "parallel")),
        name="proj",
    )(h3, g, wc, wkr, wd, ws, gcq, gckv, wuq, wuk, wuv, cos, sin_a, sin_b)


def _merge_kernel(h_ref, g_ref, ya_ref, yb_ref, yc_ref, wg_ref, woa_ref, wob_ref, woc_ref,
                  wout_ref, o_ref):
    x = h_ref[...]
    u = _rms(x, g_ref[...]).astype(BF16)
    merged = None
    for i, (y_ref, wo_ref) in enumerate(((ya_ref, woa_ref), (yb_ref, wob_ref), (yc_ref, woc_ref))):
        gate = _sigmoid(_dot(u, wg_ref[:, i * D_MODEL:(i + 1) * D_MODEL]))
        term = gate * _dot(y_ref[...], wo_ref[...])
        merged = term if merged is None else merged + term
    o_ref[...] = x + _dot(merged.astype(BF16), wout_ref[...])


def _merge(h, g, ya, yb, yc, wg, woa, wob, woc, wout):
    t = h.shape[0]
    row = pl.BlockSpec((ROW_TILE, D_MODEL), lambda i: (i, 0))
    yrow = pl.BlockSpec((ROW_TILE, ya.shape[1]), lambda i: (i, 0))
    return pl.pallas_call(
        _merge_kernel,
        grid=(t // ROW_TILE,),
        in_specs=[row, _resident(g.shape), yrow, yrow, yrow, _resident(wg.shape),
                  _resident(woa.shape), _resident(wob.shape), _resident(woc.shape),
                  _resident(wout.shape)],
        out_specs=row,
        out_shape=jax.ShapeDtypeStruct(h.shape, F32),
        compiler_params=_params(("parallel",)),
        name="merge",
    )(h, g, ya, yb, yc, wg, woa, wob, woc, wout)


def _ple_kernel(h_ref, g_ref, p_ref, wpg_ref, wpe_ref, gfin_ref, o_ref, *, final_norm):
    x = h_ref[...]
    gate = _sigmoid(_dot(_rms(x, g_ref[...]).astype(BF16), wpg_ref[...]))
    y = x + gate * _dot(p_ref[...].astype(BF16), wpe_ref[...])
    if final_norm:
        y = _rms(y, gfin_ref[...])
    o_ref[...] = y


def _ple(h, g, p, wpg, wpe, gfin, final_norm):
    t = h.shape[0]
    row = pl.BlockSpec((ROW_TILE, D_MODEL), lambda i: (i, 0))
    return pl.pallas_call(
        functools.partial(_ple_kernel, final_norm=final_norm),
        grid=(t // ROW_TILE,),
        in_specs=[row, _resident(g.shape), pl.BlockSpec((ROW_TILE, PLE_DIM), lambda i: (i, 0)),
                  _resident(wpg.shape), _resident(wpe.shape), _resident(gfin.shape)],
        out_specs=row,
        out_shape=jax.ShapeDtypeStruct(h.shape, F32),
        compiler_params=_params(("parallel",)),
        name="ple",
    )(h, g, p, wpg, wpe, gfin)


def _causal_iotas(rows, block):
    r = jnp.bitwise_and(lax.broadcasted_iota(jnp.int32, (rows, block), 0), block - 1)
    c = lax.broadcasted_iota(jnp.int32, (rows, block), 1)
    return r, c


def _softmax_step(s, v, decay, m_ref, l_ref, acc_ref):
    m_old = m_ref[...]
    if decay is not None:
        m_old = m_old - decay
    m_new = jnp.maximum(m_old, jnp.max(s, axis=-1, keepdims=True))
    alpha = jnp.exp2(m_old - m_new)
    ps = [jnp.exp2(s[:, j * LANES:(j + 1) * LANES] - m_new) for j in range(s.shape[1] // LANES)]
    l_ref[...] = alpha * l_ref[...] + functools.reduce(lambda a, b: a + b, ps)
    p = jnp.concatenate([x.astype(BF16) for x in ps], axis=1)
    acc_ref[...] = alpha * acc_ref[...] + _dot(p, v)
    m_ref[...] = m_new


def _softmax_walk(qi, scores, values, mask, decay, m_ref, l_ref, acc_ref):
    _softmax_init(m_ref, l_ref, acc_ref)

    def body(kb, carry):
        _softmax_step(scores(kb), values(kb), decay, m_ref, l_ref, acc_ref)
        return carry

    lax.fori_loop(0, qi, body, 0)
    _softmax_step(jnp.where(mask, scores(qi), NEG), values(qi), decay, m_ref, l_ref, acc_ref)
    return acc_ref[...] / jnp.sum(l_ref[...], axis=-1, keepdims=True)


def _softmax_init(m_ref, l_ref, acc_ref):
    m_ref[...] = jnp.full(m_ref.shape, NEG, F32)
    l_ref[...] = jnp.zeros(l_ref.shape, F32)
    acc_ref[...] = jnp.zeros(acc_ref.shape, F32)


def _lane_is_low(shape):
    return lax.broadcasted_iota(jnp.int32, shape, len(shape) - 1) < HALF


def _mla_kernel(q_ref, k_ref, v_ref, o_ref, m_ref, l_ref, acc_ref):
    qi = pl.program_id(2)
    q_a, q_b = q_ref[0, 0], q_ref[0, 1]

    def block(kb):
        return pl.ds(pl.multiple_of(kb * ATT_BLOCK, ATT_BLOCK), ATT_BLOCK)

    def scores(kb):
        ks = block(kb)
        return jnp.concatenate([_dot_nt(q_a, k_ref[0, 0, ks, :]), _dot_nt(q_b, k_ref[0, 1, ks, :])], axis=0)

    def values(kb):
        return v_ref[0, 0, block(kb), :]

    r, c = _causal_iotas(2 * ATT_BLOCK, ATT_BLOCK)
    o = _softmax_walk(qi, scores, values, c <= r, None, m_ref, l_ref, acc_ref)
    o_ref[0] = jnp.where(_lane_is_low((ATT_BLOCK, LANES)), o[:ATT_BLOCK], o[ATT_BLOCK:]).astype(BF16)


def _att_specs(n_q_heads, n_k_heads, seq, block):
    q = pl.BlockSpec((1, n_q_heads, block, LANES), lambda b, g, i: (b, g, i, 0))
    k = pl.BlockSpec((1, n_k_heads, seq, LANES), lambda b, g, i: (b, g, 0, 0))
    v = pl.BlockSpec((1, 1, seq, LANES), lambda b, g, i: (b, g, 0, 0))
    o = pl.BlockSpec((1, block, LANES), lambda b, g, i: (b, i, g))
    return q, k, v, o


def _att_scratch():
    rows = 2 * ATT_BLOCK
    return [pltpu.VMEM((rows, LANES), F32), pltpu.VMEM((rows, LANES), F32), pltpu.VMEM((rows, LANES), F32)]


def _mla(q, k, v):
    b, _, s, _ = q.shape
    groups = MLA_HEADS // 2
    qs, ks, vs, os_ = _att_specs(2, 2, s, ATT_BLOCK)
    return pl.pallas_call(
        _mla_kernel,
        grid=(b, groups, s // ATT_BLOCK),
        in_specs=[qs, ks, vs],
        out_specs=os_,
        out_shape=jax.ShapeDtypeStruct((b, s, groups * LANES), BF16),
        scratch_shapes=_att_scratch(),
        compiler_params=_params(("parallel", "parallel", "arbitrary")),
        name="mla",
    )(q, k, v)


def _diff_kernel(slope_ref, q_ref, k_ref, v_ref, lam_ref, gsub_ref, o_ref, m_ref, l_ref, acc_ref,
                 bias_ref, *, lam_init):
    qi = pl.program_id(2)
    q = q_ref[0, 0]
    low = _lane_is_low(q.shape)
    zero = jnp.zeros_like(q)
    qq = jnp.concatenate([jnp.where(low, q, zero), jnp.where(low, zero, q)], axis=0)
    slope = slope_ref[pl.program_id(1)] * LOG2E
    r, c = _causal_iotas(2 * ATT_BLOCK, ATT_BLOCK)

    @pl.when(qi == 0)
    def _():
        bias_ref[...] = slope * (c - r).astype(F32)

    def block(kb):
        return pl.ds(pl.multiple_of(kb * ATT_BLOCK, ATT_BLOCK), ATT_BLOCK)

    def scores(kb):
        return _dot_nt(qq, k_ref[0, 0, block(kb), :]) + bias_ref[...]

    def values(kb):
        return v_ref[0, 0, block(kb), :]

    o = _softmax_walk(qi, scores, values, c <= r, slope * ATT_BLOCK, m_ref, l_ref, acc_ref)

    lp = lam_ref[...]
    lam = (jnp.exp(jnp.sum(lp[0:1] * lp[1:2], axis=-1, keepdims=True))
           - jnp.exp(jnp.sum(lp[2:3] * lp[3:4], axis=-1, keepdims=True)) + lam_init)
    o = o[:ATT_BLOCK] - lam * o[ATT_BLOCK:]
    o_ref[0] = (_rms(o, gsub_ref[...]) * (1.0 - lam_init)).astype(BF16)


def _diff(q, k, v, slopes, lam_p, gsub, lam_init):
    b, _, s, _ = q.shape
    qs, ks, vs, os_ = _att_specs(1, 1, s, ATT_BLOCK)
    return pl.pallas_call(
        functools.partial(_diff_kernel, lam_init=lam_init),
        grid=(b, DIFF_HEADS, s // ATT_BLOCK),
        in_specs=[pl.BlockSpec(memory_space=pltpu.SMEM), qs, ks, vs,
                  pl.BlockSpec(lam_p.shape, lambda b_, g, i: (0, 0)),
                  pl.BlockSpec(gsub.shape, lambda b_, g, i: (0, 0))],
        out_specs=os_,
        out_shape=jax.ShapeDtypeStruct((b, s, DIFF_HEADS * LANES), BF16),
        scratch_shapes=_att_scratch() + [pltpu.VMEM((2 * ATT_BLOCK, ATT_BLOCK), F32)],
        compiler_params=_params(("parallel", "parallel", "arbitrary")),
        name="diff",
    )(slopes, q, k, v, lam_p, gsub)


def _sb_kernel(q_ref, k_ref, v_ref, o_ref, run_ref, acc_ref):
    qi = pl.program_id(2)
    low = _lane_is_low((SB_BLOCK, LANES))
    pair_rows = 2 * SB_BLOCK

    def stacked(q):
        zero = jnp.zeros_like(q)
        return jnp.concatenate([jnp.where(low, q, zero), jnp.where(low, zero, q)], axis=0)

    qq = [stacked(q_ref[0, p]) for p in range(SB_PAIRS)]
    r, c = _causal_iotas(SB_PAIRS * pair_rows, SB_BLOCK)
    strict = c < r
    kr = lax.broadcasted_iota(jnp.int32, (SB_BLOCK, SB_BLOCK), 0)
    kc = lax.broadcasted_iota(jnp.int32, (SB_BLOCK, SB_BLOCK), 1)
    later = (kr > kc).astype(BF16)
    run_ref[...] = jnp.zeros(run_ref.shape, F32)
    acc_ref[...] = jnp.zeros(acc_ref.shape, F32)

    def step(kb, diagonal):
        ks = pl.ds(pl.multiple_of(kb * SB_BLOCK, SB_BLOCK), SB_BLOCK)
        s = jnp.concatenate([_dot_nt(qq[p], k_ref[0, p, ks, :]) for p in range(SB_PAIRS)], axis=0)
        soft = jnp.log(1.0 + jnp.exp(-jnp.abs(s)))
        log_beta = jnp.minimum(s, 0.0) - soft
        log_1m = log_beta - s
        if diagonal:
            log_1m = jnp.where(strict, log_1m, 0.0)
        hi = log_1m.astype(BF16)
        lo = (log_1m - hi.astype(F32)).astype(BF16)
        suffix = _dot(hi, later) + _dot(lo, later)
        run = run_ref[...]
        w = jnp.exp(log_beta + suffix + jnp.concatenate([run] * (SB_BLOCK // LANES), axis=1))
        if diagonal:
            w = jnp.where(strict, w, 0.0)
        w = w.astype(BF16)
        acc_ref[...] += jnp.concatenate(
            [_dot(w[p * pair_rows:(p + 1) * pair_rows], v_ref[0, p, ks, :]) for p in range(SB_PAIRS)], axis=0)
        run = run + jnp.sum(log_1m, axis=-1, keepdims=True)
        run_ref[...] = run
        return jnp.max(run)

    top = step(qi, True)

    def cond(carry):
        kb, top = carry
        return jnp.logical_and(kb >= 0, top > SB_DEAD)

    def body(carry):
        kb, _ = carry
        return kb - 1, step(kb, False)

    lax.while_loop(cond, body, (qi - 1, top))

    o = acc_ref[...]
    o_ref[0] = jnp.concatenate(
        [jnp.where(low, o[p * pair_rows:p * pair_rows + SB_BLOCK], o[p * pair_rows + SB_BLOCK:(p + 1) * pair_rows])
         for p in range(SB_PAIRS)], axis=1).astype(BF16)


def _sb(q, k, v):
    b, pairs, s, _ = q.shape
    rows = SB_PAIRS * 2 * SB_BLOCK
    kv = pl.BlockSpec((1, SB_PAIRS, s, LANES), lambda b_, g, i: (b_, g, 0, 0))
    return pl.pallas_call(
        _sb_kernel,
        grid=(b, pairs // SB_PAIRS, s // SB_BLOCK),
        in_specs=[pl.BlockSpec((1, SB_PAIRS, SB_BLOCK, LANES), lambda b_, g, i: (b_, g, i, 0)), kv, kv],
        out_specs=pl.BlockSpec((1, SB_BLOCK, SB_PAIRS * LANES), lambda b_, g, i: (b_, i, g)),
        out_shape=jax.ShapeDtypeStruct((b, s, pairs * LANES), BF16),
        scratch_shapes=[pltpu.VMEM((rows, LANES), F32), pltpu.VMEM((rows, LANES), F32)],
        compiler_params=_params(("parallel", "parallel", "arbitrary")),
        name="sb",
    )(q, k, v)


def _rope_tables(seq):
    half = MLA_ROPE // 2
    freqs = 1.0 / (ROPE_THETA ** (jnp.arange(half, dtype=F32) / half))
    ang = jnp.arange(seq, dtype=F32)[:, None] * freqs[None, :]
    cos, sin = jnp.cos(ang), jnp.sin(ang)
    one = jnp.ones((seq, MLA_NOPE), F32)
    zn = jnp.zeros((seq, MLA_NOPE), F32)
    zh = jnp.zeros((seq, half), F32)
    zt = jnp.zeros((seq, LANES - MLA_NOPE - MLA_ROPE), F32)
    cos_t = jnp.concatenate([one, cos, cos, zt], axis=1)
    sin_a = jnp.concatenate([zn, -sin, zh, zt], axis=1)
    sin_b = jnp.concatenate([zn, zh, sin, zt], axis=1)
    return cos_t, sin_a, sin_b


def _layer_weights(w_in, w_uq, w_ukv):
    pad = LANES - MLA_NOPE - MLA_ROPE
    wc = w_in[:, :MLA_Q_LORA + MLA_KV_LORA]
    wkr = jnp.pad(w_in[:, MLA_Q_LORA + MLA_KV_LORA:MLA_IN], ((0, 0), (MLA_NOPE, pad)))
    wd = w_in[:, MLA_IN:MLA_IN + DIFF_IN]
    ws = w_in[:, MLA_IN + DIFF_IN:]
    wuq = jnp.pad(w_uq.reshape(MLA_Q_LORA, MLA_HEADS, MLA_NOPE + MLA_ROPE),
                  ((0, 0), (0, 0), (0, pad))).reshape(MLA_Q_LORA, MLA_HEADS * LANES)
    kv = w_ukv.reshape(MLA_KV_LORA, MLA_HEADS, MLA_NOPE + MLA_V)
    wuk = jnp.pad(kv[:, :, :MLA_NOPE], ((0, 0), (0, 0), (0, LANES - MLA_NOPE))).reshape(
        MLA_KV_LORA, MLA_HEADS * LANES)
    wuv = kv[:, :, MLA_NOPE:].reshape(MLA_KV_LORA, MLA_HEADS * MLA_V)
    return [w.astype(BF16) for w in (wc, wkr, wd, ws, wuq, wuk, wuv)]


def kernel(x, p, g_ffn1, w1_a, w3_a, w2_a, g_mix, w_in, g_cq, g_ckv, w_uq, w_ukv, lambda_q1, lambda_k1, lambda_q2, lambda_k2, g_subln, w_o_mla, w_o_diff, w_o_sb, w_branch_gate, w_out, g_ffn2, w1_b, w3_b, w2_b, g_ple, w_ple_gate, w_ple_proj, g_final):
    b, s, d = x.shape
    t = b * s
    assert d == D_MODEL and t % ROW_TILE == 0 and s % PROJ_TILE == 0 and s % ATT_BLOCK == 0
    cos_t, sin_a, sin_b = _rope_tables(s)
    slopes = jnp.asarray(np.exp2(-8.0 * np.arange(1, DIFF_HEADS + 1, dtype=np.float32) / DIFF_HEADS), F32)
    row = lambda v: v.reshape(1, -1)
    bf = lambda w: w.astype(BF16)

    h = x.reshape(t, d)
    for i in range(DEPTH):
        lam_init = 0.8 - 0.6 * math.exp(-0.3 * i)
        h = _ffn(h, row(g_ffn1[i]), bf(w1_a[i]), bf(w3_a[i]), bf(w2_a[i]))
        wc, wkr, wd, ws, wuq, wuk, wuv = _layer_weights(w_in[i], w_uq[i], w_ukv[i])
        qm, km, vm, qd, kd, vd, qs, ks, vs = _proj(
            h.reshape(b, s, d), row(g_mix[i]), wc, wkr, wd, ws, row(g_cq[i]), row(g_ckv[i]),
            wuq, wuk, wuv, cos_t, sin_a, sin_b)
        y_mla = _mla(qm, km, vm).reshape(t, -1)
        lam_p = jnp.stack([lambda_q1[i], lambda_k1[i], lambda_q2[i], lambda_k2[i]])
        y_diff = _diff(qd, kd, vd, slopes, lam_p, row(g_subln[i]), lam_init).reshape(t, -1)
        y_sb = _sb(qs, ks, vs).reshape(t, -1)
        h = _merge(h, row(g_mix[i]), y_mla, y_diff, y_sb, bf(w_branch_gate[i]), bf(w_o_mla[i]),
                   bf(w_o_diff[i]), bf(w_o_sb[i]), bf(w_out[i]))
        h = _ffn(h, row(g_ffn2[i]), bf(w1_b[i]), bf(w3_b[i]), bf(w2_b[i]))
        h = _ple(h, row(g_ple[i]), p[i].reshape(t, PLE_DIM), bf(w_ple_gate[i]), bf(w_ple_proj[i]),
                 row(g_final), final_norm=(i == DEPTH - 1))
    return h.reshape(b, s, d)
```

```python
import functools
import math

import jax
import jax.numpy as jnp
import numpy as np
from jax import lax
from jax.experimental import pallas as pl
from jax.experimental.pallas import tpu as pltpu

D_MODEL = 1024
DEPTH = 2
PLE_DIM = 256
EPS = 1e-6
ROPE_THETA = 10000.0

MLA_HEADS = 8
MLA_Q_LORA = 384
MLA_KV_LORA = 256
MLA_NOPE = 64
MLA_ROPE = 32
MLA_V = 64
MLA_IN = MLA_Q_LORA + MLA_KV_LORA + MLA_ROPE

DIFF_HEADS = 4
DIFF_QK = 64
DIFF_V = 2 * DIFF_QK
DIFF_QK_W = DIFF_HEADS * 2 * DIFF_QK
DIFF_IN = 2 * DIFF_QK_W + DIFF_HEADS * DIFF_V

SB_HEADS = 8
SB_DIM = 64
SB_W = SB_HEADS * SB_DIM
SB_IN = 3 * SB_W

N_BRANCH = 3
D_FF = 2816

LANES = 128
HALF = LANES // 2
VMEM_LIMIT = 56 * 1024 * 1024
NEG = -1e30
SB_DEAD = -104.0

ROW_TILE = 512
PROJ_TILE = 256
ATT_BLOCK = 512
ATT_GROUPS = 4
PAIR_ROWS = 2 * ATT_BLOCK
SB_BLOCK = 256
SB_PAIRS = 4
LOG2E = math.log2(math.e)
MLA_Q_SCALE = (MLA_NOPE + MLA_ROPE) ** -0.5 * LOG2E
DIFF_Q_SCALE = DIFF_QK ** -0.5 * LOG2E

BF16 = jnp.bfloat16
F32 = jnp.float32


def _dot(a, b):
    return jnp.dot(a, b, preferred_element_type=F32)


def _dot_nt(a, b):
    return lax.dot_general(a, b, (((1,), (1,)), ((), ())), preferred_element_type=F32)


def _rms(x, g):
    ms = jnp.mean(x * x, axis=-1, keepdims=True)
    return x * lax.rsqrt(ms + EPS) * g


def _sigmoid(x):
    return 1.0 / (1.0 + jnp.exp(-x))


def _resident(shape):
    nd = len(shape)
    return pl.BlockSpec(shape, lambda *_: (0,) * nd, pipeline_mode=pl.Buffered(1))


def _params(sem):
    return pltpu.CompilerParams(dimension_semantics=sem, vmem_limit_bytes=VMEM_LIMIT)


def _ffn_kernel(h_ref, g_ref, w1_ref, w3_ref, w2_ref, o_ref):
    x = h_ref[...]
    n = _rms(x, g_ref[...]).astype(BF16)
    a = _dot(n, w1_ref[...])
    b = _dot(n, w3_ref[...])
    act = (a * _sigmoid(a) * b).astype(BF16)
    o_ref[...] = x + 0.5 * _dot(act, w2_ref[...])


def _ffn(h, g, w1, w3, w2):
    t = h.shape[0]
    row = pl.BlockSpec((ROW_TILE, D_MODEL), lambda i: (i, 0))
    return pl.pallas_call(
        _ffn_kernel,
        grid=(t // ROW_TILE,),
        in_specs=[row, _resident((1, D_MODEL)), _resident(w1.shape), _resident(w3.shape),
                  _resident(w2.shape)],
        out_specs=row,
        out_shape=jax.ShapeDtypeStruct(h.shape, F32),
        compiler_params=_params(("parallel",)),
        name="ffn",
    )(h, g, w1, w3, w2)


def _rope(x, cos, sin_a, sin_b):
    return (x * cos + pltpu.roll(x, LANES - MLA_ROPE // 2, 1) * sin_a
            + pltpu.roll(x, MLA_ROPE // 2, 1) * sin_b)


def _proj_kernel(h_ref, g_ref, wc_ref, wkr_ref, wd_ref, ws_ref, gcq_ref, gckv_ref, wuq_ref,
                 wuk_ref, wuv_ref, cos_ref, sa_ref, sb_ref,
                 qm_ref, km_ref, vm_ref, qd_ref, kd_ref, vd_ref, qs_ref, ks_ref, vs_ref):
    u = _rms(h_ref[0], g_ref[...]).astype(BF16)
    cos, sin_a, sin_b = cos_ref[...], sa_ref[...], sb_ref[...]

    c = _dot(u, wc_ref[...])
    cq = _rms(c[:, :MLA_Q_LORA], gcq_ref[...]).astype(BF16)
    ckv = _rms(c[:, MLA_Q_LORA:], gckv_ref[...]).astype(BF16)
    q = _dot(cq, wuq_ref[...])
    kn = _dot(ckv, wuk_ref[...])
    v = _dot(ckv, wuv_ref[...])
    kr = _rope(_dot(u, wkr_ref[...]), cos, sin_a, sin_b)
    for hd in range(MLA_HEADS):
        sl = slice(hd * LANES, (hd + 1) * LANES)
        qm_ref[0, hd] = (_rope(q[:, sl], cos, sin_a, sin_b) * MLA_Q_SCALE).astype(BF16)
        km_ref[0, hd] = (kn[:, sl] + kr).astype(BF16)
    for p in range(MLA_HEADS // 2):
        vm_ref[0, p] = v[:, p * LANES:(p + 1) * LANES].astype(BF16)

    d = _dot(u, wd_ref[...])
    for hd in range(DIFF_HEADS):
        sl = slice(hd * LANES, (hd + 1) * LANES)
        qd_ref[0, hd] = (d[:, sl] * DIFF_Q_SCALE).astype(BF16)
        kd_ref[0, hd] = d[:, DIFF_QK_W + hd * LANES:DIFF_QK_W + (hd + 1) * LANES].astype(BF16)
        vd_ref[0, hd] = d[:, 2 * DIFF_QK_W + hd * LANES:2 * DIFF_QK_W + (hd + 1) * LANES].astype(BF16)

    s = _dot(u, ws_ref[...])
    for p in range(SB_HEADS // 2):
        sl = slice(p * LANES, (p + 1) * LANES)
        qs_ref[0, p] = (s[:, sl] * SB_DIM ** -0.5).astype(BF16)
        ks_ref[0, p] = s[:, SB_W + p * LANES:SB_W + (p + 1) * LANES].astype(BF16)
        vs_ref[0, p] = s[:, 2 * SB_W + p * LANES:2 * SB_W + (p + 1) * LANES].astype(BF16)


def _proj(h3, g, wc, wkr, wd, ws, gcq, gckv, wuq, wuk, wuv, cos, sin_a, sin_b):
    b, s, _ = h3.shape
    tab = pl.BlockSpec((PROJ_TILE, LANES), lambda bi, si: (si, 0))

    def heads(n):
        return pl.BlockSpec((1, n, PROJ_TILE, LANES), lambda bi, si: (bi, 0, si, 0))

    def out(n):
        return jax.ShapeDtypeStruct((b, n, s, LANES), BF16)

    n_out = [MLA_HEADS, MLA_HEADS, MLA_HEADS // 2, DIFF_HEADS, DIFF_HEADS, DIFF_HEADS,
             SB_HEADS // 2, SB_HEADS // 2, SB_HEADS // 2]
    return pl.pallas_call(
        _proj_kernel,
        grid=(b, s // PROJ_TILE),
        in_specs=[pl.BlockSpec((1, PROJ_TILE, D_MODEL), lambda bi, si: (bi, si, 0)),
                  _resident(g.shape), _resident(wc.shape), _resident(wkr.shape),
                  _resident(wd.shape), _resident(ws.shape), _resident(gcq.shape),
                  _resident(gckv.shape), _resident(wuq.shape), _resident(wuk.shape),
                  _resident(wuv.shape), tab, tab, tab],
        out_specs=[heads(n) for n in n_out],
        out_shape=[out(n) for n in n_out],
        compiler_params=_params(("parallel", "parallel")),
        name="proj",
    )(h3, g, wc, wkr, wd, ws, gcq, gckv, wuq, wuk, wuv, cos, sin_a, sin_b)


def _merge_kernel(h_ref, g_ref, ya_ref, yb_ref, yc_ref, wg_ref, woa_ref, wob_ref, woc_ref,
                  wout_ref, o_ref):
    x = h_ref[...]
    u = _rms(x, g_ref[...]).astype(BF16)
    merged = None
    for i, (y_ref, wo_ref) in enumerate(((ya_ref, woa_ref), (yb_ref, wob_ref), (yc_ref, woc_ref))):
        gate = _sigmoid(_dot(u, wg_ref[:, i * D_MODEL:(i + 1) * D_MODEL]))
        term = gate * _dot(y_ref[...], wo_ref[...])
        merged = term if merged is None else merged + term
    o_ref[...] = x + _dot(merged.astype(BF16), wout_ref[...])


def _merge(h, g, ya, yb, yc, wg, woa, wob, woc, wout):
    t = h.shape[0]
    row = pl.BlockSpec((ROW_TILE, D_MODEL), lambda i: (i, 0))
    yrow = pl.BlockSpec((ROW_TILE, ya.shape[1]), lambda i: (i, 0))
    return pl.pallas_call(
        _merge_kernel,
        grid=(t // ROW_TILE,),
        in_specs=[row, _resident(g.shape), yrow, yrow, yrow, _resident(wg.shape),
                  _resident(woa.shape), _resident(wob.shape), _resident(woc.shape),
                  _resident(wout.shape)],
        out_specs=row,
        out_shape=jax.ShapeDtypeStruct(h.shape, F32),
        compiler_params=_params(("parallel",)),
        name="merge",
    )(h, g, ya, yb, yc, wg, woa, wob, woc, wout)


def _ple_kernel(h_ref, g_ref, p_ref, wpg_ref, wpe_ref, gfin_ref, o_ref, *, final_norm):
    x = h_ref[...]
    gate = _sigmoid(_dot(_rms(x, g_ref[...]).astype(BF16), wpg_ref[...]))
    y = x + gate * _dot(p_ref[...].astype(BF16), wpe_ref[...])
    if final_norm:
        y = _rms(y, gfin_ref[...])
    o_ref[...] = y


def _ple(h, g, p, wpg, wpe, gfin, final_norm):
    t = h.shape[0]
    row = pl.BlockSpec((ROW_TILE, D_MODEL), lambda i: (i, 0))
    return pl.pallas_call(
        functools.partial(_ple_kernel, final_norm=final_norm),
        grid=(t // ROW_TILE,),
        in_specs=[row, _resident(g.shape), pl.BlockSpec((ROW_TILE, PLE_DIM), lambda i: (i, 0)),
                  _resident(wpg.shape), _resident(wpe.shape), _resident(gfin.shape)],
        out_specs=row,
        out_shape=jax.ShapeDtypeStruct(h.shape, F32),
        compiler_params=_params(("parallel",)),
        name="ple",
    )(h, g, p, wpg, wpe, gfin)


def _causal_iotas(rows, block):
    r = jnp.bitwise_and(lax.broadcasted_iota(jnp.int32, (rows, block), 0), block - 1)
    c = lax.broadcasted_iota(jnp.int32, (rows, block), 1)
    return r, c


def _softmax_step(s, pv, decay, m_ref, l_ref, acc_ref):
    m_old = m_ref[...]
    if decay is not None:
        m_old = m_old - decay
    m_new = jnp.maximum(m_old, jnp.max(s, axis=-1, keepdims=True))
    alpha = jnp.exp2(m_old - m_new)
    ps = [jnp.exp2(s[:, j * LANES:(j + 1) * LANES] - m_new) for j in range(s.shape[1] // LANES)]
    l_ref[...] = alpha * l_ref[...] + functools.reduce(lambda a, b: a + b, ps)
    p = jnp.concatenate([x.astype(BF16) for x in ps], axis=1)
    acc_ref[...] = alpha * acc_ref[...] + pv(p)
    m_ref[...] = m_new


def _softmax_walk(qi, scores, weighted_values, mask, decay, m_ref, l_ref, acc_ref):
    _softmax_init(m_ref, l_ref, acc_ref)

    def body(kb, carry):
        _softmax_step(scores(kb), functools.partial(weighted_values, kb), decay, m_ref, l_ref, acc_ref)
        return carry

    lax.fori_loop(0, qi, body, 0)
    _softmax_step(jnp.where(mask, scores(qi), NEG), functools.partial(weighted_values, qi), decay,
                  m_ref, l_ref, acc_ref)
    return acc_ref[...] / jnp.sum(l_ref[...], axis=-1, keepdims=True)


def _softmax_init(m_ref, l_ref, acc_ref):
    m_ref[...] = jnp.full(m_ref.shape, NEG, F32)
    l_ref[...] = jnp.zeros(l_ref.shape, F32)
    acc_ref[...] = jnp.zeros(acc_ref.shape, F32)


def _lane_is_low(shape):
    return lax.broadcasted_iota(jnp.int32, shape, len(shape) - 1) < HALF


def _mla_kernel(q_ref, k_ref, v_ref, o_ref, m_ref, l_ref, acc_ref):
    qi = pl.program_id(2)
    heads = 2 * ATT_GROUPS
    qs = [q_ref[0, hd] for hd in range(heads)]

    def block(kb):
        return pl.ds(pl.multiple_of(kb * ATT_BLOCK, ATT_BLOCK), ATT_BLOCK)

    def scores(kb):
        ks = block(kb)
        return jnp.concatenate([_dot_nt(qs[hd], k_ref[0, hd, ks, :]) for hd in range(heads)], axis=0)

    def weighted_values(kb, p):
        ks = block(kb)
        return jnp.concatenate([_dot(p[g * PAIR_ROWS:(g + 1) * PAIR_ROWS], v_ref[0, g, ks, :])
                                for g in range(ATT_GROUPS)], axis=0)

    r, c = _causal_iotas(ATT_GROUPS * PAIR_ROWS, ATT_BLOCK)
    o = _softmax_walk(qi, scores, weighted_values, c <= r, None, m_ref, l_ref, acc_ref)
    low = _lane_is_low((ATT_BLOCK, LANES))
    o_ref[0] = jnp.concatenate(
        [jnp.where(low, o[g * PAIR_ROWS:g * PAIR_ROWS + ATT_BLOCK], o[g * PAIR_ROWS + ATT_BLOCK:(g + 1) * PAIR_ROWS])
         for g in range(ATT_GROUPS)], axis=1).astype(BF16)


def _att_specs(heads_per_group, seq):
    q = pl.BlockSpec((1, ATT_GROUPS * heads_per_group, ATT_BLOCK, LANES), lambda b, g, i: (b, g, i, 0))
    k = pl.BlockSpec((1, ATT_GROUPS * heads_per_group, seq, LANES), lambda b, g, i: (b, g, 0, 0))
    v = pl.BlockSpec((1, ATT_GROUPS, seq, LANES), lambda b, g, i: (b, g, 0, 0))
    o = pl.BlockSpec((1, ATT_BLOCK, ATT_GROUPS * LANES), lambda b, g, i: (b, i, g))
    return q, k, v, o


def _att_scratch():
    rows = ATT_GROUPS * PAIR_ROWS
    return [pltpu.VMEM((rows, LANES), F32), pltpu.VMEM((rows, LANES), F32), pltpu.VMEM((rows, LANES), F32)]


def _mla(q, k, v):
    b, _, s, _ = q.shape
    groups = MLA_HEADS // 2
    qs, ks, vs, os_ = _att_specs(2, s)
    return pl.pallas_call(
        _mla_kernel,
        grid=(b, groups // ATT_GROUPS, s // ATT_BLOCK),
        in_specs=[qs, ks, vs],
        out_specs=os_,
        out_shape=jax.ShapeDtypeStruct((b, s, groups * LANES), BF16),
        scratch_shapes=_att_scratch(),
        compiler_params=_params(("parallel", "parallel", "arbitrary")),
        name="mla",
    )(q, k, v)


def _diff_kernel(slope_ref, q_ref, k_ref, v_ref, lam_ref, gsub_ref, o_ref, m_ref, l_ref, acc_ref,
                 bias_ref, decay_ref, *, lam_init):
    qi = pl.program_id(2)
    low = _lane_is_low((ATT_BLOCK, LANES))

    def stacked(q):
        zero = jnp.zeros_like(q)
        return jnp.concatenate([jnp.where(low, q, zero), jnp.where(low, zero, q)], axis=0)

    qq = [stacked(q_ref[0, g]) for g in range(ATT_GROUPS)]

    @pl.when(qi == 0)
    def _():
        r1, c1 = _causal_iotas(PAIR_ROWS, ATT_BLOCK)
        rel = (c1 - r1).astype(F32)
        for g in range(ATT_GROUPS):
            slope = slope_ref[pl.program_id(1) * ATT_GROUPS + g] * LOG2E
            bias_ref[g * PAIR_ROWS:(g + 1) * PAIR_ROWS, :] = slope * rel
            decay_ref[g * PAIR_ROWS:(g + 1) * PAIR_ROWS, :] = jnp.full((PAIR_ROWS, LANES), slope * ATT_BLOCK, F32)

    def block(kb):
        return pl.ds(pl.multiple_of(kb * ATT_BLOCK, ATT_BLOCK), ATT_BLOCK)

    def scores(kb):
        ks = block(kb)
        s = jnp.concatenate([_dot_nt(qq[g], k_ref[0, g, ks, :]) for g in range(ATT_GROUPS)], axis=0)
        return s + bias_ref[...]

    def weighted_values(kb, p):
        ks = block(kb)
        return jnp.concatenate([_dot(p[g * PAIR_ROWS:(g + 1) * PAIR_ROWS], v_ref[0, g, ks, :])
                                for g in range(ATT_GROUPS)], axis=0)

    r, c = _causal_iotas(ATT_GROUPS * PAIR_ROWS, ATT_BLOCK)
    o = _softmax_walk(qi, scores, weighted_values, c <= r, decay_ref[...], m_ref, l_ref, acc_ref)

    lp = lam_ref[...]
    lam = (jnp.exp(jnp.sum(lp[0:1] * lp[1:2], axis=-1, keepdims=True))
           - jnp.exp(jnp.sum(lp[2:3] * lp[3:4], axis=-1, keepdims=True)) + lam_init)
    outs = []
    for g in range(ATT_GROUPS):
        og = o[g * PAIR_ROWS:g * PAIR_ROWS + ATT_BLOCK] - lam * o[g * PAIR_ROWS + ATT_BLOCK:(g + 1) * PAIR_ROWS]
        outs.append((_rms(og, gsub_ref[...]) * (1.0 - lam_init)).astype(BF16))
    o_ref[0] = jnp.concatenate(outs, axis=1)


def _diff(q, k, v, slopes, lam_p, gsub, lam_init):
    b, _, s, _ = q.shape
    qs, ks, vs, os_ = _att_specs(1, s)
    rows = ATT_GROUPS * PAIR_ROWS
    return pl.pallas_call(
        functools.partial(_diff_kernel, lam_init=lam_init),
        grid=(b, DIFF_HEADS // ATT_GROUPS, s // ATT_BLOCK),
        in_specs=[pl.BlockSpec(memory_space=pltpu.SMEM), qs, ks, vs,
                  pl.BlockSpec(lam_p.shape, lambda b_, g, i: (0, 0)),
                  pl.BlockSpec(gsub.shape, lambda b_, g, i: (0, 0))],
        out_specs=os_,
        out_shape=jax.ShapeDtypeStruct((b, s, DIFF_HEADS * LANES), BF16),
        scratch_shapes=_att_scratch() + [pltpu.VMEM((rows, ATT_BLOCK), F32), pltpu.VMEM((rows, LANES), F32)],
        compiler_params=_params(("parallel", "parallel", "arbitrary")),
        name="diff",
    )(slopes, q, k, v, lam_p, gsub)


def _sb_kernel(q_ref, k_ref, v_ref, o_ref, run_ref, acc_ref):
    qi = pl.program_id(2)
    low = _lane_is_low((SB_BLOCK, LANES))
    pair_rows = 2 * SB_BLOCK

    def stacked(q):
        zero = jnp.zeros_like(q)
        return jnp.concatenate([jnp.where(low, q, zero), jnp.where(low, zero, q)], axis=0)

    qq = [stacked(q_ref[0, p]) for p in range(SB_PAIRS)]
    r, c = _causal_iotas(SB_PAIRS * pair_rows, SB_BLOCK)
    strict = c < r
    kr = lax.broadcasted_iota(jnp.int32, (SB_BLOCK, SB_BLOCK), 0)
    kc = lax.broadcasted_iota(jnp.int32, (SB_BLOCK, SB_BLOCK), 1)
    later = (kr > kc).astype(BF16)
    run_ref[...] = jnp.zeros(run_ref.shape, F32)
    acc_ref[...] = jnp.zeros(acc_ref.shape, F32)

    def step(kb, diagonal):
        ks = pl.ds(pl.multiple_of(kb * SB_BLOCK, SB_BLOCK), SB_BLOCK)
        s = jnp.concatenate([_dot_nt(qq[p], k_ref[0, p, ks, :]) for p in range(SB_PAIRS)], axis=0)
        soft = jnp.log(1.0 + jnp.exp(-jnp.abs(s)))
        log_beta = jnp.minimum(s, 0.0) - soft
        log_1m = log_beta - s
        if diagonal:
            log_1m = jnp.where(strict, log_1m, 0.0)
        hi = log_1m.astype(BF16)
        lo = (log_1m - hi.astype(F32)).astype(BF16)
        suffix = _dot(hi, later) + _dot(lo, later)
        run = run_ref[...]
        w = jnp.exp(log_beta + suffix + jnp.concatenate([run] * (SB_BLOCK // LANES), axis=1))
        if diagonal:
            w = jnp.where(strict, w, 0.0)
        w = w.astype(BF16)
        acc_ref[...] += jnp.concatenate(
            [_dot(w[p * pair_rows:(p + 1) * pair_rows], v_ref[0, p, ks, :]) for p in range(SB_PAIRS)], axis=0)
        run = run + jnp.sum(log_1m, axis=-1, keepdims=True)
        run_ref[...] = run
        return jnp.max(run)

    top = step(qi, True)

    def cond(carry):
        kb, top = carry
        return jnp.logical_and(kb >= 0, top > SB_DEAD)

    def body(carry):
        kb, _ = carry
        return kb - 1, step(kb, False)

    lax.while_loop(cond, body, (qi - 1, top))

    o = acc_ref[...]
    o_ref[0] = jnp.concatenate(
        [jnp.where(low, o[p * pair_rows:p * pair_rows + SB_BLOCK], o[p * pair_rows + SB_BLOCK:(p + 1) * pair_rows])
         for p in range(SB_PAIRS)], axis=1).astype(BF16)


def _sb(q, k, v):
    b, pairs, s, _ = q.shape
    rows = SB_PAIRS * 2 * SB_BLOCK
    kv = pl.BlockSpec((1, SB_PAIRS, s, LANES), lambda b_, g, i: (b_, g, 0, 0))
    return pl.pallas_call(
        _sb_kernel,
        grid=(b, pairs // SB_PAIRS, s // SB_BLOCK),
        in_specs=[pl.BlockSpec((1, SB_PAIRS, SB_BLOCK, LANES), lambda b_, g, i: (b_, g, i, 0)), kv, kv],
        out_specs=pl.BlockSpec((1, SB_BLOCK, SB_PAIRS * LANES), lambda b_, g, i: (b_, i, g)),
        out_shape=jax.ShapeDtypeStruct((b, s, pairs * LANES), BF16),
        scratch_shapes=[pltpu.VMEM((rows, LANES), F32), pltpu.VMEM((rows, LANES), F32)],
        compiler_params=_params(("parallel", "parallel", "arbitrary")),
        name="sb",
    )(q, k, v)


def _rope_tables(seq):
    half = MLA_ROPE // 2
    freqs = 1.0 / (ROPE_THETA ** (jnp.arange(half, dtype=F32) / half))
    ang = jnp.arange(seq, dtype=F32)[:, None] * freqs[None, :]
    cos, sin = jnp.cos(ang), jnp.sin(ang)
    one = jnp.ones((seq, MLA_NOPE), F32)
    zn = jnp.zeros((seq, MLA_NOPE), F32)
    zh = jnp.zeros((seq, half), F32)
    zt = jnp.zeros((seq, LANES - MLA_NOPE - MLA_ROPE), F32)
    cos_t = jnp.concatenate([one, cos, cos, zt], axis=1)
    sin_a = jnp.concatenate([zn, -sin, zh, zt], axis=1)
    sin_b = jnp.concatenate([zn, zh, sin, zt], axis=1)
    return cos_t, sin_a, sin_b


def _layer_weights(w_in, w_uq, w_ukv):
    pad = LANES - MLA_NOPE - MLA_ROPE
    wc = w_in[:, :MLA_Q_LORA + MLA_KV_LORA]
    wkr = jnp.pad(w_in[:, MLA_Q_LORA + MLA_KV_LORA:MLA_IN], ((0, 0), (MLA_NOPE, pad)))
    wd = w_in[:, MLA_IN:MLA_IN + DIFF_IN]
    ws = w_in[:, MLA_IN + DIFF_IN:]
    wuq = jnp.pad(w_uq.reshape(MLA_Q_LORA, MLA_HEADS, MLA_NOPE + MLA_ROPE),
                  ((0, 0), (0, 0), (0, pad))).reshape(MLA_Q_LORA, MLA_HEADS * LANES)
    kv = w_ukv.reshape(MLA_KV_LORA, MLA_HEADS, MLA_NOPE + MLA_V)
    wuk = jnp.pad(kv[:, :, :MLA_NOPE], ((0, 0), (0, 0), (0, LANES - MLA_NOPE))).reshape(
        MLA_KV_LORA, MLA_HEADS * LANES)
    wuv = kv[:, :, MLA_NOPE:].reshape(MLA_KV_LORA, MLA_HEADS * MLA_V)
    return [w.astype(BF16) for w in (wc, wkr, wd, ws, wuq, wuk, wuv)]


def kernel(x, p, g_ffn1, w1_a, w3_a, w2_a, g_mix, w_in, g_cq, g_ckv, w_uq, w_ukv, lambda_q1, lambda_k1, lambda_q2, lambda_k2, g_subln, w_o_mla, w_o_diff, w_o_sb, w_branch_gate, w_out, g_ffn2, w1_b, w3_b, w2_b, g_ple, w_ple_gate, w_ple_proj, g_final):
    b, s, d = x.shape
    t = b * s
    assert d == D_MODEL and t % ROW_TILE == 0 and s % PROJ_TILE == 0 and s % ATT_BLOCK == 0
    cos_t, sin_a, sin_b = _rope_tables(s)
    slopes = jnp.asarray(np.exp2(-8.0 * np.arange(1, DIFF_HEADS + 1, dtype=np.float32) / DIFF_HEADS), F32)
    row = lambda v: v.reshape(1, -1)
    bf = lambda w: w.astype(BF16)

    h = x.reshape(t, d)
    for i in range(DEPTH):
        lam_init = 0.8 - 0.6 * math.exp(-0.3 * i)
        h = _ffn(h, row(g_ffn1[i]), bf(w1_a[i]), bf(w3_a[i]), bf(w2_a[i]))
        wc, wkr, wd, ws, wuq, wuk, wuv = _layer_weights(w_in[i], w_uq[i], w_ukv[i])
        qm, km, vm, qd, kd, vd, qs, ks, vs = _proj(
            h.reshape(b, s, d), row(g_mix[i]), wc, wkr, wd, ws, row(g_cq[i]), row(g_ckv[i]),
            wuq, wuk, wuv, cos_t, sin_a, sin_b)
        y_mla = _mla(qm, km, vm).reshape(t, -1)
        lam_p = jnp.stack([lambda_q1[i], lambda_k1[i], lambda_q2[i], lambda_k2[i]])
        y_diff = _diff(qd, kd, vd, slopes, lam_p, row(g_subln[i]), lam_init).reshape(t, -1)
        y_sb = _sb(qs, ks, vs).reshape(t, -1)
        h = _merge(h, row(g_mix[i]), y_mla, y_diff, y_sb, bf(w_branch_gate[i]), bf(w_o_mla[i]),
                   bf(w_o_diff[i]), bf(w_o_sb[i]), bf(w_out[i]))
        h = _ffn(h, row(g_ffn2[i]), bf(w1_b[i]), bf(w3_b[i]), bf(w2_b[i]))
        h = _ple(h, row(g_ple[i]), p[i].reshape(t, PLE_DIM), bf(w_ple_gate[i]), bf(w_ple_proj[i]),
                 row(g_final), final_norm=(i == DEPTH - 1))
    return h.reshape(b, s, d)
```

```python
import functools
import math

import jax
import jax.numpy as jnp
import numpy as np
from jax import lax
from jax.experimental import pallas as pl
from jax.experimental.pallas import tpu as pltpu

D_MODEL = 1024
DEPTH = 2
PLE_DIM = 256
EPS = 1e-6
ROPE_THETA = 10000.0

MLA_HEADS = 8
MLA_Q_LORA = 384
MLA_KV_LORA = 256
MLA_NOPE = 64
MLA_ROPE = 32
MLA_V = 64
MLA_IN = MLA_Q_LORA + MLA_KV_LORA + MLA_ROPE

DIFF_HEADS = 4
DIFF_QK = 64
DIFF_V = 2 * DIFF_QK
DIFF_QK_W = DIFF_HEADS * 2 * DIFF_QK
DIFF_IN = 2 * DIFF_QK_W + DIFF_HEADS * DIFF_V

SB_HEADS = 8
SB_DIM = 64
SB_W = SB_HEADS * SB_DIM
SB_IN = 3 * SB_W

N_BRANCH = 3
D_FF = 2816

LANES = 128
HALF = LANES // 2
VMEM_LIMIT = 56 * 1024 * 1024
CAST_BLOCK_BYTES = 4 * 1024 * 1024
BF16_SUBLANES = 16
NEG = -1e30
SB_DEAD = -104.0

ROW_TILE = 512
PROJ_TILE = 512
ATT_BLOCK = 512
ATT_GROUPS = 4
PAIR_ROWS = 2 * ATT_BLOCK
SB_BLOCK = 256
SB_PAIRS = 4
LOG2E = math.log2(math.e)
MLA_Q_SCALE = (MLA_NOPE + MLA_ROPE) ** -0.5 * LOG2E
DIFF_Q_SCALE = DIFF_QK ** -0.5 * LOG2E

BF16 = jnp.bfloat16
F32 = jnp.float32


def _dot(a, b):
    return jnp.dot(a, b, preferred_element_type=F32)


def _dot_nt(a, b):
    return lax.dot_general(a, b, (((1,), (1,)), ((), ())), preferred_element_type=F32)


def _rms(x, g):
    ms = jnp.mean(x * x, axis=-1, keepdims=True)
    return x * lax.rsqrt(ms + EPS) * g


def _sigmoid(x):
    return 1.0 / (1.0 + jnp.exp(-x))


def _resident(shape):
    nd = len(shape)
    return pl.BlockSpec(shape, lambda *_: (0,) * nd, pipeline_mode=pl.Buffered(1))


def _layer(w, layer):
    return pl.BlockSpec((None,) + w.shape[1:], lambda *_: (layer, 0, 0), pipeline_mode=pl.Buffered(1))


def _params(sem):
    return pltpu.CompilerParams(dimension_semantics=sem, vmem_limit_bytes=VMEM_LIMIT)


def _cast_kernel(x_ref, o_ref):
    o_ref[...] = x_ref[...].astype(BF16)


def _to_bf16(w):
    x = w.reshape(-1, w.shape[-1])
    rows, n = x.shape
    tile = rows
    while tile * n * 4 > CAST_BLOCK_BYTES and tile % (2 * BF16_SUBLANES) == 0:
        tile //= 2
    blk = pl.BlockSpec((tile, n), lambda i: (i, 0))
    out = pl.pallas_call(
        _cast_kernel,
        grid=(rows // tile,),
        in_specs=[blk],
        out_specs=blk,
        out_shape=jax.ShapeDtypeStruct(x.shape, BF16),
        compiler_params=_params(("parallel",)),
        name="cast",
    )(x)
    return out.reshape(w.shape)


def _ffn_kernel(h_ref, g_ref, w1_ref, w3_ref, w2_ref, o_ref):
    x = h_ref[...]
    n = _rms(x, g_ref[...]).astype(BF16)
    a = _dot(n, w1_ref[...])
    b = _dot(n, w3_ref[...])
    act = (a * _sigmoid(a) * b).astype(BF16)
    o_ref[...] = x + 0.5 * _dot(act, w2_ref[...])


def _ffn(h, g, w1, w3, w2, layer):
    t = h.shape[0]
    row = pl.BlockSpec((ROW_TILE, D_MODEL), lambda i: (i, 0))
    return pl.pallas_call(
        _ffn_kernel,
        grid=(t // ROW_TILE,),
        in_specs=[row, _resident((1, D_MODEL)), _layer(w1, layer), _layer(w3, layer), _layer(w2, layer)],
        out_specs=row,
        out_shape=jax.ShapeDtypeStruct(h.shape, F32),
        compiler_params=_params(("parallel",)),
        name="ffn",
    )(h, g, w1, w3, w2)


def _rope(x, cos, sin_a, sin_b):
    return (x * cos + pltpu.roll(x, LANES - MLA_ROPE // 2, 1) * sin_a
            + pltpu.roll(x, MLA_ROPE // 2, 1) * sin_b)


def _proj_kernel(h_ref, g_ref, wc_ref, wkr_ref, wd_ref, ws_ref, gcq_ref, gckv_ref, wuq_ref,
                 wuk_ref, wuv_ref, cos_ref, sa_ref, sb_ref,
                 qm_ref, km_ref, vm_ref, qd_ref, kd_ref, vd_ref, qs_ref, ks_ref, vs_ref):
    u = _rms(h_ref[0], g_ref[...]).astype(BF16)
    cos, sin_a, sin_b = cos_ref[...], sa_ref[...], sb_ref[...]

    c = _dot(u, wc_ref[...])
    cq = _rms(c[:, :MLA_Q_LORA], gcq_ref[...]).astype(BF16)
    ckv = _rms(c[:, MLA_Q_LORA:], gckv_ref[...]).astype(BF16)
    q = _dot(cq, wuq_ref[...])
    kn = _dot(ckv, wuk_ref[...])
    v = _dot(ckv, wuv_ref[...])
    kr = _rope(_dot(u, wkr_ref[...]), cos, sin_a, sin_b)
    for hd in range(MLA_HEADS):
        sl = slice(hd * LANES, (hd + 1) * LANES)
        qm_ref[0, hd] = (_rope(q[:, sl], cos, sin_a, sin_b) * MLA_Q_SCALE).astype(BF16)
        km_ref[0, hd] = (kn[:, sl] + kr).astype(BF16)
    for p in range(MLA_HEADS // 2):
        vm_ref[0, p] = v[:, p * LANES:(p + 1) * LANES].astype(BF16)

    d = _dot(u, wd_ref[...])
    for hd in range(DIFF_HEADS):
        sl = slice(hd * LANES, (hd + 1) * LANES)
        qd_ref[0, hd] = (d[:, sl] * DIFF_Q_SCALE).astype(BF16)
        kd_ref[0, hd] = d[:, DIFF_QK_W + hd * LANES:DIFF_QK_W + (hd + 1) * LANES].astype(BF16)
        vd_ref[0, hd] = d[:, 2 * DIFF_QK_W + hd * LANES:2 * DIFF_QK_W + (hd + 1) * LANES].astype(BF16)

    s = _dot(u, ws_ref[...])
    for p in range(SB_HEADS // 2):
        sl = slice(p * LANES, (p + 1) * LANES)
        qs_ref[0, p] = (s[:, sl] * SB_DIM ** -0.5).astype(BF16)
        ks_ref[0, p] = s[:, SB_W + p * LANES:SB_W + (p + 1) * LANES].astype(BF16)
        vs_ref[0, p] = s[:, 2 * SB_W + p * LANES:2 * SB_W + (p + 1) * LANES].astype(BF16)


def _proj(h3, g, wc, wkr, wd, ws, gcq, gckv, wuq, wuk, wuv, cos, sin_a, sin_b):
    b, s, _ = h3.shape
    tab = pl.BlockSpec((PROJ_TILE, LANES), lambda bi, si: (si, 0))

    def heads(n):
        return pl.BlockSpec((1, n, PROJ_TILE, LANES), lambda bi, si: (bi, 0, si, 0))

    def out(n):
        return jax.ShapeDtypeStruct((b, n, s, LANES), BF16)

    n_out = [MLA_HEADS, MLA_HEADS, MLA_HEADS // 2, DIFF_HEADS, DIFF_HEADS, DIFF_HEADS,
             SB_HEADS // 2, SB_HEADS // 2, SB_HEADS // 2]
    return pl.pallas_call(
        _proj_kernel,
        grid=(b, s // PROJ_TILE),
        in_specs=[pl.BlockSpec((1, PROJ_TILE, D_MODEL), lambda bi, si: (bi, si, 0)),
                  _resident(g.shape), _resident(wc.shape), _resident(wkr.shape),
                  _resident(wd.shape), _resident(ws.shape), _resident(gcq.shape),
                  _resident(gckv.shape), _resident(wuq.shape), _resident(wuk.shape),
                  _resident(wuv.shape), tab, tab, tab],
        out_specs=[heads(n) for n in n_out],
        out_shape=[out(n) for n in n_out],
        compiler_params=_params(("parallel", "parallel")),
        name="proj",
    )(h3, g, wc, wkr, wd, ws, gcq, gckv, wuq, wuk, wuv, cos, sin_a, sin_b)


def _merge_kernel(h_ref, g_ref, ya_ref, yb_ref, yc_ref, wg_ref, woa_ref, wob_ref, woc_ref,
                  wout_ref, o_ref):
    x = h_ref[...]
    u = _rms(x, g_ref[...]).astype(BF16)
    merged = None
    for i, (y_ref, wo_ref) in enumerate(((ya_ref, woa_ref), (yb_ref, wob_ref), (yc_ref, woc_ref))):
        gate = _sigmoid(_dot(u, wg_ref[:, i * D_MODEL:(i + 1) * D_MODEL]))
        term = gate * _dot(y_ref[...], wo_ref[...])
        merged = term if merged is None else merged + term
    o_ref[...] = x + _dot(merged.astype(BF16), wout_ref[...])


def _merge(h, g, ya, yb, yc, wg, woa, wob, woc, wout, layer):
    t = h.shape[0]
    row = pl.BlockSpec((ROW_TILE, D_MODEL), lambda i: (i, 0))
    yrow = pl.BlockSpec((ROW_TILE, ya.shape[1]), lambda i: (i, 0))
    return pl.pallas_call(
        _merge_kernel,
        grid=(t // ROW_TILE,),
        in_specs=[row, _resident(g.shape), yrow, yrow, yrow, _layer(wg, layer), _layer(woa, layer),
                  _layer(wob, layer), _layer(woc, layer), _layer(wout, layer)],
        out_specs=row,
        out_shape=jax.ShapeDtypeStruct(h.shape, F32),
        compiler_params=_params(("parallel",)),
        name="merge",
    )(h, g, ya, yb, yc, wg, woa, wob, woc, wout)


def _ple_kernel(h_ref, g_ref, p_ref, wpg_ref, wpe_ref, gfin_ref, o_ref, *, final_norm):
    x = h_ref[...]
    gate = _sigmoid(_dot(_rms(x, g_ref[...]).astype(BF16), wpg_ref[...]))
    y = x + gate * _dot(p_ref[...].astype(BF16), wpe_ref[...])
    if final_norm:
        y = _rms(y, gfin_ref[...])
    o_ref[...] = y


def _ple(h, g, p, wpg, wpe, gfin, layer, final_norm):
    t = h.shape[0]
    row = pl.BlockSpec((ROW_TILE, D_MODEL), lambda i: (i, 0))
    return pl.pallas_call(
        functools.partial(_ple_kernel, final_norm=final_norm),
        grid=(t // ROW_TILE,),
        in_specs=[row, _resident(g.shape), pl.BlockSpec((ROW_TILE, PLE_DIM), lambda i: (i, 0)),
                  _layer(wpg, layer), _layer(wpe, layer), _resident(gfin.shape)],
        out_specs=row,
        out_shape=jax.ShapeDtypeStruct(h.shape, F32),
        compiler_params=_params(("parallel",)),
        name="ple",
    )(h, g, p, wpg, wpe, gfin)


def _causal_iotas(rows, block):
    r = jnp.bitwise_and(lax.broadcasted_iota(jnp.int32, (rows, block), 0), block - 1)
    c = lax.broadcasted_iota(jnp.int32, (rows, block), 1)
    return r, c


def _softmax_step(s, pv, decay, m_ref, l_ref, acc_ref):
    m_old = m_ref[...]
    if decay is not None:
        m_old = m_old - decay
    m_new = jnp.maximum(m_old, jnp.max(s, axis=-1, keepdims=True))
    alpha = jnp.exp2(m_old - m_new)
    ps = [jnp.exp2(s[:, j * LANES:(j + 1) * LANES] - m_new) for j in range(s.shape[1] // LANES)]
    l_ref[...] = alpha * l_ref[...] + functools.reduce(lambda a, b: a + b, ps)
    p = jnp.concatenate([x.astype(BF16) for x in ps], axis=1)
    acc_ref[...] = alpha * acc_ref[...] + pv(p)
    m_ref[...] = m_new


def _softmax_walk(qi, scores, weighted_values, mask, decay, m_ref, l_ref, acc_ref):
    _softmax_init(m_ref, l_ref, acc_ref)

    def body(kb, carry):
        _softmax_step(scores(kb), functools.partial(weighted_values, kb), decay, m_ref, l_ref, acc_ref)
        return carry

    lax.fori_loop(0, qi, body, 0)
    _softmax_step(jnp.where(mask, scores(qi), NEG), functools.partial(weighted_values, qi), decay,
                  m_ref, l_ref, acc_ref)
    return acc_ref[...] / jnp.sum(l_ref[...], axis=-1, keepdims=True)


def _softmax_init(m_ref, l_ref, acc_ref):
    m_ref[...] = jnp.full(m_ref.shape, NEG, F32)
    l_ref[...] = jnp.zeros(l_ref.shape, F32)
    acc_ref[...] = jnp.zeros(acc_ref.shape, F32)


def _lane_is_low(shape):
    return lax.broadcasted_iota(jnp.int32, shape, len(shape) - 1) < HALF


def _mla_kernel(q_ref, k_ref, v_ref, o_ref, m_ref, l_ref, acc_ref):
    qi = pl.program_id(2)
    heads = 2 * ATT_GROUPS
    qs = [q_ref[0, hd] for hd in range(heads)]

    def block(kb):
        return pl.ds(pl.multiple_of(kb * ATT_BLOCK, ATT_BLOCK), ATT_BLOCK)

    def scores(kb):
        ks = block(kb)
        return jnp.concatenate([_dot_nt(qs[hd], k_ref[0, hd, ks, :]) for hd in range(heads)], axis=0)

    def weighted_values(kb, p):
        ks = block(kb)
        return jnp.concatenate([_dot(p[g * PAIR_ROWS:(g + 1) * PAIR_ROWS], v_ref[0, g, ks, :])
                                for g in range(ATT_GROUPS)], axis=0)

    r, c = _causal_iotas(ATT_GROUPS * PAIR_ROWS, ATT_BLOCK)
    o = _softmax_walk(qi, scores, weighted_values, c <= r, None, m_ref, l_ref, acc_ref)
    low = _lane_is_low((ATT_BLOCK, LANES))
    o_ref[0] = jnp.concatenate(
        [jnp.where(low, o[g * PAIR_ROWS:g * PAIR_ROWS + ATT_BLOCK], o[g * PAIR_ROWS + ATT_BLOCK:(g + 1) * PAIR_ROWS])
         for g in range(ATT_GROUPS)], axis=1).astype(BF16)


def _att_specs(heads_per_group, seq):
    q = pl.BlockSpec((1, ATT_GROUPS * heads_per_group, ATT_BLOCK, LANES), lambda b, g, i: (b, g, i, 0))
    k = pl.BlockSpec((1, ATT_GROUPS * heads_per_group, seq, LANES), lambda b, g, i: (b, g, 0, 0))
    v = pl.BlockSpec((1, ATT_GROUPS, seq, LANES), lambda b, g, i: (b, g, 0, 0))
    o = pl.BlockSpec((1, ATT_BLOCK, ATT_GROUPS * LANES), lambda b, g, i: (b, i, g))
    return q, k, v, o


def _att_scratch():
    rows = ATT_GROUPS * PAIR_ROWS
    return [pltpu.VMEM((rows, LANES), F32), pltpu.VMEM((rows, LANES), F32), pltpu.VMEM((rows, LANES), F32)]


def _mla(q, k, v):
    b, _, s, _ = q.shape
    groups = MLA_HEADS // 2
    qs, ks, vs, os_ = _att_specs(2, s)
    return pl.pallas_call(
        _mla_kernel,
        grid=(b, groups // ATT_GROUPS, s // ATT_BLOCK),
        in_specs=[qs, ks, vs],
        out_specs=os_,
        out_shape=jax.ShapeDtypeStruct((b, s, groups * LANES), BF16),
        scratch_shapes=_att_scratch(),
        compiler_params=_params(("parallel", "parallel", "arbitrary")),
        name="mla",
    )(q, k, v)


def _diff_kernel(slope_ref, q_ref, k_ref, v_ref, lam_ref, gsub_ref, o_ref, m_ref, l_ref, acc_ref,
                 bias_ref, decay_ref, *, lam_init):
    qi = pl.program_id(2)
    low = _lane_is_low((ATT_BLOCK, LANES))

    def stacked(q):
        zero = jnp.zeros_like(q)
        return jnp.concatenate([jnp.where(low, q, zero), jnp.where(low, zero, q)], axis=0)

    qq = [stacked(q_ref[0, g]) for g in range(ATT_GROUPS)]

    @pl.when(qi == 0)
    def _():
        r1, c1 = _causal_iotas(PAIR_ROWS, ATT_BLOCK)
        rel = (c1 - r1).astype(F32)
        for g in range(ATT_GROUPS):
            slope = slope_ref[pl.program_id(1) * ATT_GROUPS + g] * LOG2E
            bias_ref[g * PAIR_ROWS:(g + 1) * PAIR_ROWS, :] = slope * rel
            decay_ref[g * PAIR_ROWS:(g + 1) * PAIR_ROWS, :] = jnp.full((PAIR_ROWS, LANES), slope * ATT_BLOCK, F32)

    def block(kb):
        return pl.ds(pl.multiple_of(kb * ATT_BLOCK, ATT_BLOCK), ATT_BLOCK)

    def scores(kb):
        ks = block(kb)
        s = jnp.concatenate([_dot_nt(qq[g], k_ref[0, g, ks, :]) for g in range(ATT_GROUPS)], axis=0)
        return s + bias_ref[...]

    def weighted_values(kb, p):
        ks = block(kb)
        return jnp.concatenate([_dot(p[g * PAIR_ROWS:(g + 1) * PAIR_ROWS], v_ref[0, g, ks, :])
                                for g in range(ATT_GROUPS)], axis=0)

    r, c = _causal_iotas(ATT_GROUPS * PAIR_ROWS, ATT_BLOCK)
    o = _softmax_walk(qi, scores, weighted_values, c <= r, decay_ref[...], m_ref, l_ref, acc_ref)

    lp = lam_ref[...]
    lam = (jnp.exp(jnp.sum(lp[0:1] * lp[1:2], axis=-1, keepdims=True))
           - jnp.exp(jnp.sum(lp[2:3] * lp[3:4], axis=-1, keepdims=True)) + lam_init)
    outs = []
    for g in range(ATT_GROUPS):
        og = o[g * PAIR_ROWS:g * PAIR_ROWS + ATT_BLOCK] - lam * o[g * PAIR_ROWS + ATT_BLOCK:(g + 1) * PAIR_ROWS]
        outs.append((_rms(og, gsub_ref[...]) * (1.0 - lam_init)).astype(BF16))
    o_ref[0] = jnp.concatenate(outs, axis=1)


def _diff(q, k, v, slopes, lam_p, gsub, lam_init):
    b, _, s, _ = q.shape
    qs, ks, vs, os_ = _att_specs(1, s)
    rows = ATT_GROUPS * PAIR_ROWS
    return pl.pallas_call(
        functools.partial(_diff_kernel, lam_init=lam_init),
        grid=(b, DIFF_HEADS // ATT_GROUPS, s // ATT_BLOCK),
        in_specs=[pl.BlockSpec(memory_space=pltpu.SMEM), qs, ks, vs,
                  pl.BlockSpec(lam_p.shape, lambda b_, g, i: (0, 0)),
                  pl.BlockSpec(gsub.shape, lambda b_, g, i: (0, 0))],
        out_specs=os_,
        out_shape=jax.ShapeDtypeStruct((b, s, DIFF_HEADS * LANES), BF16),
        scratch_shapes=_att_scratch() + [pltpu.VMEM((rows, ATT_BLOCK), F32), pltpu.VMEM((rows, LANES), F32)],
        compiler_params=_params(("parallel", "parallel", "arbitrary")),
        name="diff",
    )(slopes, q, k, v, lam_p, gsub)


def _sb_kernel(q_ref, k_ref, v_ref, o_ref, run_ref, acc_ref):
    qi = pl.program_id(2)
    low = _lane_is_low((SB_BLOCK, LANES))
    pair_rows = 2 * SB_BLOCK

    def stacked(q):
        zero = jnp.zeros_like(q)
        return jnp.concatenate([jnp.where(low, q, zero), jnp.where(low, zero, q)], axis=0)

    qq = [stacked(q_ref[0, p]) for p in range(SB_PAIRS)]
    r, c = _causal_iotas(SB_PAIRS * pair_rows, SB_BLOCK)
    strict = c < r
    kr = lax.broadcasted_iota(jnp.int32, (SB_BLOCK, SB_BLOCK), 0)
    kc = lax.broadcasted_iota(jnp.int32, (SB_BLOCK, SB_BLOCK), 1)
    later = (kr > kc).astype(BF16)
    run_ref[...] = jnp.zeros(run_ref.shape, F32)
    acc_ref[...] = jnp.zeros(acc_ref.shape, F32)

    def step(kb, diagonal):
        ks = pl.ds(pl.multiple_of(kb * SB_BLOCK, SB_BLOCK), SB_BLOCK)
        s = jnp.concatenate([_dot_nt(qq[p], k_ref[0, p, ks, :]) for p in range(SB_PAIRS)], axis=0)
        soft = jnp.log(1.0 + jnp.exp(-jnp.abs(s)))
        log_beta = jnp.minimum(s, 0.0) - soft
        log_1m = log_beta - s
        if diagonal:
            log_1m = jnp.where(strict, log_1m, 0.0)
        hi = log_1m.astype(BF16)
        lo = (log_1m - hi.astype(F32)).astype(BF16)
        suffix = _dot(hi, later) + _dot(lo, later)
        run = run_ref[...]
        w = jnp.exp(log_beta + suffix + jnp.concatenate([run] * (SB_BLOCK // LANES), axis=1))
        if diagonal:
            w = jnp.where(strict, w, 0.0)
        w = w.astype(BF16)
        acc_ref[...] += jnp.concatenate(
            [_dot(w[p * pair_rows:(p + 1) * pair_rows], v_ref[0, p, ks, :]) for p in range(SB_PAIRS)], axis=0)
        run = run + jnp.sum(log_1m, axis=-1, keepdims=True)
        run_ref[...] = run
        return jnp.max(run)

    top = step(qi, True)

    def cond(carry):
        kb, top = carry
        return jnp.logical_and(kb >= 0, top > SB_DEAD)

    def body(carry):
        kb, _ = carry
        return kb - 1, step(kb, False)

    lax.while_loop(cond, body, (qi - 1, top))

    o = acc_ref[...]
    o_ref[0] = jnp.concatenate(
        [jnp.where(low, o[p * pair_rows:p * pair_rows + SB_BLOCK], o[p * pair_rows + SB_BLOCK:(p + 1) * pair_rows])
         for p in range(SB_PAIRS)], axis=1).astype(BF16)


def _sb(q, k, v):
    b, pairs, s, _ = q.shape
    rows = SB_PAIRS * 2 * SB_BLOCK
    kv = pl.BlockSpec((1, SB_PAIRS, s, LANES), lambda b_, g, i: (b_, g, 0, 0))
    return pl.pallas_call(
        _sb_kernel,
        grid=(b, pairs // SB_PAIRS, s // SB_BLOCK),
        in_specs=[pl.BlockSpec((1, SB_PAIRS, SB_BLOCK, LANES), lambda b_, g, i: (b_, g, i, 0)), kv, kv],
        out_specs=pl.BlockSpec((1, SB_BLOCK, SB_PAIRS * LANES), lambda b_, g, i: (b_, i, g)),
        out_shape=jax.ShapeDtypeStruct((b, s, pairs * LANES), BF16),
        scratch_shapes=[pltpu.VMEM((rows, LANES), F32), pltpu.VMEM((rows, LANES), F32)],
        compiler_params=_params(("parallel", "parallel", "arbitrary")),
        name="sb",
    )(q, k, v)


def _rope_tables(seq):
    half = MLA_ROPE // 2
    freqs = 1.0 / (ROPE_THETA ** (jnp.arange(half, dtype=F32) / half))
    ang = jnp.arange(seq, dtype=F32)[:, None] * freqs[None, :]
    cos, sin = jnp.cos(ang), jnp.sin(ang)
    one = jnp.ones((seq, MLA_NOPE), F32)
    zn = jnp.zeros((seq, MLA_NOPE), F32)
    zh = jnp.zeros((seq, half), F32)
    zt = jnp.zeros((seq, LANES - MLA_NOPE - MLA_ROPE), F32)
    cos_t = jnp.concatenate([one, cos, cos, zt], axis=1)
    sin_a = jnp.concatenate([zn, -sin, zh, zt], axis=1)
    sin_b = jnp.concatenate([zn, zh, sin, zt], axis=1)
    return cos_t, sin_a, sin_b


def _layer_weights(w_in, w_uq, w_ukv):
    pad = LANES - MLA_NOPE - MLA_ROPE
    wc = w_in[:, :MLA_Q_LORA + MLA_KV_LORA]
    wkr = jnp.pad(w_in[:, MLA_Q_LORA + MLA_KV_LORA:MLA_IN], ((0, 0), (MLA_NOPE, pad)))
    wd = w_in[:, MLA_IN:MLA_IN + DIFF_IN]
    ws = w_in[:, MLA_IN + DIFF_IN:]
    wuq = jnp.pad(w_uq.reshape(MLA_Q_LORA, MLA_HEADS, MLA_NOPE + MLA_ROPE),
                  ((0, 0), (0, 0), (0, pad))).reshape(MLA_Q_LORA, MLA_HEADS * LANES)
    kv = w_ukv.reshape(MLA_KV_LORA, MLA_HEADS, MLA_NOPE + MLA_V)
    wuk = jnp.pad(kv[:, :, :MLA_NOPE], ((0, 0), (0, 0), (0, LANES - MLA_NOPE))).reshape(
        MLA_KV_LORA, MLA_HEADS * LANES)
    wuv = kv[:, :, MLA_NOPE:].reshape(MLA_KV_LORA, MLA_HEADS * MLA_V)
    return [w.astype(BF16) for w in (wc, wkr, wd, ws, wuq, wuk, wuv)]


def kernel(x, p, g_ffn1, w1_a, w3_a, w2_a, g_mix, w_in, g_cq, g_ckv, w_uq, w_ukv, lambda_q1, lambda_k1, lambda_q2, lambda_k2, g_subln, w_o_mla, w_o_diff, w_o_sb, w_branch_gate, w_out, g_ffn2, w1_b, w3_b, w2_b, g_ple, w_ple_gate, w_ple_proj, g_final):
    b, s, d = x.shape
    t = b * s
    assert d == D_MODEL and t % ROW_TILE == 0 and s % PROJ_TILE == 0 and s % ATT_BLOCK == 0
    cos_t, sin_a, sin_b = _rope_tables(s)
    slopes = jnp.asarray(np.exp2(-8.0 * np.arange(1, DIFF_HEADS + 1, dtype=np.float32) / DIFF_HEADS), F32)
    row = lambda v: v.reshape(1, -1)
    w1_a, w3_a, w2_a, w1_b, w3_b, w2_b = map(_to_bf16, (w1_a, w3_a, w2_a, w1_b, w3_b, w2_b))
    w_in, w_uq, w_ukv = map(_to_bf16, (w_in, w_uq, w_ukv))
    w_branch_gate, w_o_mla, w_o_diff, w_o_sb, w_out = map(
        _to_bf16, (w_branch_gate, w_o_mla, w_o_diff, w_o_sb, w_out))
    w_ple_gate, w_ple_proj = map(_to_bf16, (w_ple_gate, w_ple_proj))

    h = x.reshape(t, d)
    for i in range(DEPTH):
        lam_init = 0.8 - 0.6 * math.exp(-0.3 * i)
        h = _ffn(h, row(g_ffn1[i]), w1_a, w3_a, w2_a, i)
        wc, wkr, wd, ws, wuq, wuk, wuv = _layer_weights(w_in[i], w_uq[i], w_ukv[i])
        qm, km, vm, qd, kd, vd, qs, ks, vs = _proj(
            h.reshape(b, s, d), row(g_mix[i]), wc, wkr, wd, ws, row(g_cq[i]), row(g_ckv[i]),
            wuq, wuk, wuv, cos_t, sin_a, sin_b)
        y_mla = _mla(qm, km, vm).reshape(t, -1)
        lam_p = jnp.stack([lambda_q1[i], lambda_k1[i], lambda_q2[i], lambda_k2[i]])
        y_diff = _diff(qd, kd, vd, slopes, lam_p, row(g_subln[i]), lam_init).reshape(t, -1)
        y_sb = _sb(qs, ks, vs).reshape(t, -1)
        h = _merge(h, row(g_mix[i]), y_mla, y_diff, y_sb, w_branch_gate, w_o_mla, w_o_diff, w_o_sb, w_out, i)
        h = _ffn(h, row(g_ffn2[i]), w1_b, w3_b, w2_b, i)
        h = _ple(h, row(g_ple[i]), p[i].reshape(t, PLE_DIM), w_ple_gate, w_ple_proj, row(g_final), i,
                 final_norm=(i == DEPTH - 1))
    return h.reshape(b, s, d)
```

```python
import functools
import math

import jax
import jax.numpy as jnp
import numpy as np
from jax import lax
from jax.experimental import pallas as pl
from jax.experimental.pallas import tpu as pltpu

D_MODEL = 1024
DEPTH = 2
PLE_DIM = 256
EPS = 1e-6
ROPE_THETA = 10000.0

MLA_HEADS = 8
MLA_Q_LORA = 384
MLA_KV_LORA = 256
MLA_NOPE = 64
MLA_ROPE = 32
MLA_V = 64
MLA_IN = MLA_Q_LORA + MLA_KV_LORA + MLA_ROPE

DIFF_HEADS = 4
DIFF_QK = 64
DIFF_V = 2 * DIFF_QK
DIFF_QK_W = DIFF_HEADS * 2 * DIFF_QK
DIFF_IN = 2 * DIFF_QK_W + DIFF_HEADS * DIFF_V

SB_HEADS = 8
SB_DIM = 64
SB_W = SB_HEADS * SB_DIM
SB_IN = 3 * SB_W

N_BRANCH = 3
D_FF = 2816

LANES = 128
HALF = LANES // 2
VMEM_LIMIT = 56 * 1024 * 1024
CAST_BLOCK_BYTES = 4 * 1024 * 1024
BF16_SUBLANES = 16
NEG = -1e30
SB_DEAD = -104.0

ROW_TILE = 512
PROJ_TILE = 512
ATT_BLOCK = 512
ATT_GROUPS = 4
PAIR_ROWS = 2 * ATT_BLOCK
SB_BLOCK = 256
SB_PAIRS = 4
LOG2E = math.log2(math.e)
MLA_Q_SCALE = (MLA_NOPE + MLA_ROPE) ** -0.5 * LOG2E
DIFF_Q_SCALE = DIFF_QK ** -0.5 * LOG2E

BF16 = jnp.bfloat16
F32 = jnp.float32


def _dot(a, b):
    return jnp.dot(a, b, preferred_element_type=F32)


def _dot_nt(a, b):
    return lax.dot_general(a, b, (((1,), (1,)), ((), ())), preferred_element_type=F32)


def _rms(x, g):
    ms = jnp.mean(x * x, axis=-1, keepdims=True)
    return x * lax.rsqrt(ms + EPS) * g


def _sigmoid(x):
    return 1.0 / (1.0 + jnp.exp(-x))


def _resident(shape):
    nd = len(shape)
    return pl.BlockSpec(shape, lambda *_: (0,) * nd, pipeline_mode=pl.Buffered(1))


def _layer(w, layer):
    return pl.BlockSpec((None,) + w.shape[1:], lambda *_: (layer, 0, 0), pipeline_mode=pl.Buffered(1))


def _params(sem):
    return pltpu.CompilerParams(dimension_semantics=sem, vmem_limit_bytes=VMEM_LIMIT)


def _cast_kernel(x_ref, o_ref):
    o_ref[...] = x_ref[...].astype(BF16)


def _to_bf16(w):
    x = w.reshape(-1, w.shape[-1])
    rows, n = x.shape
    tile = rows
    while tile * n * 4 > CAST_BLOCK_BYTES and tile % (2 * BF16_SUBLANES) == 0:
        tile //= 2
    blk = pl.BlockSpec((tile, n), lambda i: (i, 0))
    out = pl.pallas_call(
        _cast_kernel,
        grid=(rows // tile,),
        in_specs=[blk],
        out_specs=blk,
        out_shape=jax.ShapeDtypeStruct(x.shape, BF16),
        compiler_params=_params(("parallel",)),
        name="cast",
    )(x)
    return out.reshape(w.shape)


def _half_ffn(x, g_ref, w1_ref, w3_ref, w2_ref):
    n = _rms(x, g_ref[...]).astype(BF16)
    a = _dot(n, w1_ref[...])
    b = _dot(n, w3_ref[...])
    act = (a * _sigmoid(a) * b).astype(BF16)
    return x + 0.5 * _dot(act, w2_ref[...])


def _ffn_kernel(h_ref, g_ref, w1_ref, w3_ref, w2_ref, o_ref):
    o_ref[...] = _half_ffn(h_ref[...], g_ref, w1_ref, w3_ref, w2_ref)


def _ffn_ple_kernel(h_ref, g_ref, w1_ref, w3_ref, w2_ref, gp_ref, p_ref, wpg_ref, wpe_ref, gfin_ref,
                    o_ref, *, final_norm):
    y = _half_ffn(h_ref[...], g_ref, w1_ref, w3_ref, w2_ref)
    gate = _sigmoid(_dot(_rms(y, gp_ref[...]).astype(BF16), wpg_ref[...]))
    y = y + gate * _dot(p_ref[...].astype(BF16), wpe_ref[...])
    if final_norm:
        y = _rms(y, gfin_ref[...])
    o_ref[...] = y


def _ffn_ple(h, g, w1, w3, w2, gp, p, wpg, wpe, gfin, layer, final_norm):
    t = h.shape[0]
    row = pl.BlockSpec((ROW_TILE, D_MODEL), lambda i: (i, 0))
    return pl.pallas_call(
        functools.partial(_ffn_ple_kernel, final_norm=final_norm),
        grid=(t // ROW_TILE,),
        in_specs=[row, _resident(g.shape), _layer(w1, layer), _layer(w3, layer), _layer(w2, layer),
                  _resident(gp.shape), pl.BlockSpec((ROW_TILE, PLE_DIM), lambda i: (i, 0)),
                  _layer(wpg, layer), _layer(wpe, layer), _resident(gfin.shape)],
        out_specs=row,
        out_shape=jax.ShapeDtypeStruct(h.shape, F32),
        compiler_params=_params(("parallel",)),
        name="ffn_ple",
    )(h, g, w1, w3, w2, gp, p, wpg, wpe, gfin)


def _ffn(h, g, w1, w3, w2, layer):
    t = h.shape[0]
    row = pl.BlockSpec((ROW_TILE, D_MODEL), lambda i: (i, 0))
    return pl.pallas_call(
        _ffn_kernel,
        grid=(t // ROW_TILE,),
        in_specs=[row, _resident((1, D_MODEL)), _layer(w1, layer), _layer(w3, layer), _layer(w2, layer)],
        out_specs=row,
        out_shape=jax.ShapeDtypeStruct(h.shape, F32),
        compiler_params=_params(("parallel",)),
        name="ffn",
    )(h, g, w1, w3, w2)


def _rope(x, cos, sin_a, sin_b):
    return (x * cos + pltpu.roll(x, LANES - MLA_ROPE // 2, 1) * sin_a
            + pltpu.roll(x, MLA_ROPE // 2, 1) * sin_b)


def _proj_kernel(h_ref, g_ref, wc_ref, wkr_ref, wd_ref, ws_ref, gcq_ref, gckv_ref, wuq_ref,
                 wuk_ref, wuv_ref, cos_ref, sa_ref, sb_ref,
                 qm_ref, km_ref, vm_ref, qd_ref, kd_ref, vd_ref, qs_ref, ks_ref, vs_ref):
    u = _rms(h_ref[0], g_ref[...]).astype(BF16)
    cos, sin_a, sin_b = cos_ref[...], sa_ref[...], sb_ref[...]

    c = _dot(u, wc_ref[...])
    cq = _rms(c[:, :MLA_Q_LORA], gcq_ref[...]).astype(BF16)
    ckv = _rms(c[:, MLA_Q_LORA:], gckv_ref[...]).astype(BF16)
    q = _dot(cq, wuq_ref[...])
    kn = _dot(ckv, wuk_ref[...])
    v = _dot(ckv, wuv_ref[...])
    kr = _rope(_dot(u, wkr_ref[...]), cos, sin_a, sin_b)
    for hd in range(MLA_HEADS):
        sl = slice(hd * LANES, (hd + 1) * LANES)
        qm_ref[0, hd] = (_rope(q[:, sl], cos, sin_a, sin_b) * MLA_Q_SCALE).astype(BF16)
        km_ref[0, hd] = (kn[:, sl] + kr).astype(BF16)
    for p in range(MLA_HEADS // 2):
        vm_ref[0, p] = v[:, p * LANES:(p + 1) * LANES].astype(BF16)

    d = _dot(u, wd_ref[...])
    for hd in range(DIFF_HEADS):
        sl = slice(hd * LANES, (hd + 1) * LANES)
        qd_ref[0, hd] = (d[:, sl] * DIFF_Q_SCALE).astype(BF16)
        kd_ref[0, hd] = d[:, DIFF_QK_W + hd * LANES:DIFF_QK_W + (hd + 1) * LANES].astype(BF16)
        vd_ref[0, hd] = d[:, 2 * DIFF_QK_W + hd * LANES:2 * DIFF_QK_W + (hd + 1) * LANES].astype(BF16)

    s = _dot(u, ws_ref[...])
    for p in range(SB_HEADS // 2):
        sl = slice(p * LANES, (p + 1) * LANES)
        qs_ref[0, p] = (s[:, sl] * SB_DIM ** -0.5).astype(BF16)
        ks_ref[0, p] = s[:, SB_W + p * LANES:SB_W + (p + 1) * LANES].astype(BF16)
        vs_ref[0, p] = s[:, 2 * SB_W + p * LANES:2 * SB_W + (p + 1) * LANES].astype(BF16)


def _proj(h3, g, wc, wkr, wd, ws, gcq, gckv, wuq, wuk, wuv, cos, sin_a, sin_b):
    b, s, _ = h3.shape
    tab = pl.BlockSpec((PROJ_TILE, LANES), lambda bi, si: (si, 0))

    def heads(n):
        return pl.BlockSpec((1, n, PROJ_TILE, LANES), lambda bi, si: (bi, 0, si, 0))

    def out(n):
        return jax.ShapeDtypeStruct((b, n, s, LANES), BF16)

    n_out = [MLA_HEADS, MLA_HEADS, MLA_HEADS // 2, DIFF_HEADS, DIFF_HEADS, DIFF_HEADS,
             SB_HEADS // 2, SB_HEADS // 2, SB_HEADS // 2]
    return pl.pallas_call(
        _proj_kernel,
        grid=(b, s // PROJ_TILE),
        in_specs=[pl.BlockSpec((1, PROJ_TILE, D_MODEL), lambda bi, si: (bi, si, 0)),
                  _resident(g.shape), _resident(wc.shape), _resident(wkr.shape),
                  _resident(wd.shape), _resident(ws.shape), _resident(gcq.shape),
                  _resident(gckv.shape), _resident(wuq.shape), _resident(wuk.shape),
                  _resident(wuv.shape), tab, tab, tab],
        out_specs=[heads(n) for n in n_out],
        out_shape=[out(n) for n in n_out],
        compiler_params=_params(("parallel", "parallel")),
        name="proj",
    )(h3, g, wc, wkr, wd, ws, gcq, gckv, wuq, wuk, wuv, cos, sin_a, sin_b)


def _merge_kernel(h_ref, g_ref, ya_ref, yb_ref, yc_ref, wg_ref, woa_ref, wob_ref, woc_ref,
                  wout_ref, o_ref):
    x = h_ref[...]
    u = _rms(x, g_ref[...]).astype(BF16)
    merged = None
    for i, (y_ref, wo_ref) in enumerate(((ya_ref, woa_ref), (yb_ref, wob_ref), (yc_ref, woc_ref))):
        gate = _sigmoid(_dot(u, wg_ref[:, i * D_MODEL:(i + 1) * D_MODEL]))
        term = gate * _dot(y_ref[...], wo_ref[...])
        merged = term if merged is None else merged + term
    o_ref[...] = x + _dot(merged.astype(BF16), wout_ref[...])


def _merge(h, g, ya, yb, yc, wg, woa, wob, woc, wout, layer):
    t = h.shape[0]
    row = pl.BlockSpec((ROW_TILE, D_MODEL), lambda i: (i, 0))
    yrow = pl.BlockSpec((ROW_TILE, ya.shape[1]), lambda i: (i, 0))
    return pl.pallas_call(
        _merge_kernel,
        grid=(t // ROW_TILE,),
        in_specs=[row, _resident(g.shape), yrow, yrow, yrow, _layer(wg, layer), _layer(woa, layer),
                  _layer(wob, layer), _layer(woc, layer), _layer(wout, layer)],
        out_specs=row,
        out_shape=jax.ShapeDtypeStruct(h.shape, F32),
        compiler_params=_params(("parallel",)),
        name="merge",
    )(h, g, ya, yb, yc, wg, woa, wob, woc, wout)


def _causal_iotas(rows, block):
    r = jnp.bitwise_and(lax.broadcasted_iota(jnp.int32, (rows, block), 0), block - 1)
    c = lax.broadcasted_iota(jnp.int32, (rows, block), 1)
    return r, c


def _softmax_step(s, pv, decay, m_ref, l_ref, acc_ref):
    m_old = m_ref[...]
    if decay is not None:
        m_old = m_old - decay
    m_new = jnp.maximum(m_old, jnp.max(s, axis=-1, keepdims=True))
    alpha = jnp.exp2(m_old - m_new)
    ps = [jnp.exp2(s[:, j * LANES:(j + 1) * LANES] - m_new) for j in range(s.shape[1] // LANES)]
    l_ref[...] = alpha * l_ref[...] + functools.reduce(lambda a, b: a + b, ps)
    p = jnp.concatenate([x.astype(BF16) for x in ps], axis=1)
    acc_ref[...] = alpha * acc_ref[...] + pv(p)
    m_ref[...] = m_new


def _softmax_walk(qi, scores, weighted_values, mask, decay, m_ref, l_ref, acc_ref):
    _softmax_init(m_ref, l_ref, acc_ref)

    def body(kb, carry):
        _softmax_step(scores(kb), functools.partial(weighted_values, kb), decay, m_ref, l_ref, acc_ref)
        return carry

    lax.fori_loop(0, qi, body, 0)
    _softmax_step(jnp.where(mask, scores(qi), NEG), functools.partial(weighted_values, qi), decay,
                  m_ref, l_ref, acc_ref)
    return acc_ref[...] / jnp.sum(l_ref[...], axis=-1, keepdims=True)


def _softmax_init(m_ref, l_ref, acc_ref):
    m_ref[...] = jnp.full(m_ref.shape, NEG, F32)
    l_ref[...] = jnp.zeros(l_ref.shape, F32)
    acc_ref[...] = jnp.zeros(acc_ref.shape, F32)


def _lane_is_low(shape):
    return lax.broadcasted_iota(jnp.int32, shape, len(shape) - 1) < HALF


def _mla_kernel(q_ref, k_ref, v_ref, o_ref, m_ref, l_ref, acc_ref):
    qi = pl.program_id(2)
    heads = 2 * ATT_GROUPS
    qs = [q_ref[0, hd] for hd in range(heads)]

    def block(kb):
        return pl.ds(pl.multiple_of(kb * ATT_BLOCK, ATT_BLOCK), ATT_BLOCK)

    def scores(kb):
        ks = block(kb)
        return jnp.concatenate([_dot_nt(qs[hd], k_ref[0, hd, ks, :]) for hd in range(heads)], axis=0)

    def weighted_values(kb, p):
        ks = block(kb)
        return jnp.concatenate([_dot(p[g * PAIR_ROWS:(g + 1) * PAIR_ROWS], v_ref[0, g, ks, :])
                                for g in range(ATT_GROUPS)], axis=0)

    r, c = _causal_iotas(ATT_GROUPS * PAIR_ROWS, ATT_BLOCK)
    o = _softmax_walk(qi, scores, weighted_values, c <= r, None, m_ref, l_ref, acc_ref)
    low = _lane_is_low((ATT_BLOCK, LANES))
    o_ref[0] = jnp.concatenate(
        [jnp.where(low, o[g * PAIR_ROWS:g * PAIR_ROWS + ATT_BLOCK], o[g * PAIR_ROWS + ATT_BLOCK:(g + 1) * PAIR_ROWS])
         for g in range(ATT_GROUPS)], axis=1).astype(BF16)


def _att_specs(heads_per_group, seq):
    q = pl.BlockSpec((1, ATT_GROUPS * heads_per_group, ATT_BLOCK, LANES), lambda b, g, i: (b, g, i, 0))
    k = pl.BlockSpec((1, ATT_GROUPS * heads_per_group, seq, LANES), lambda b, g, i: (b, g, 0, 0))
    v = pl.BlockSpec((1, ATT_GROUPS, seq, LANES), lambda b, g, i: (b, g, 0, 0))
    o = pl.BlockSpec((1, ATT_BLOCK, ATT_GROUPS * LANES), lambda b, g, i: (b, i, g))
    return q, k, v, o


def _att_scratch():
    rows = ATT_GROUPS * PAIR_ROWS
    return [pltpu.VMEM((rows, LANES), F32), pltpu.VMEM((rows, LANES), F32), pltpu.VMEM((rows, LANES), F32)]


def _mla(q, k, v):
    b, _, s, _ = q.shape
    groups = MLA_HEADS // 2
    qs, ks, vs, os_ = _att_specs(2, s)
    return pl.pallas_call(
        _mla_kernel,
        grid=(b, groups // ATT_GROUPS, s // ATT_BLOCK),
        in_specs=[qs, ks, vs],
        out_specs=os_,
        out_shape=jax.ShapeDtypeStruct((b, s, groups * LANES), BF16),
        scratch_shapes=_att_scratch(),
        compiler_params=_params(("parallel", "parallel", "arbitrary")),
        name="mla",
    )(q, k, v)


def _diff_kernel(slope_ref, q_ref, k_ref, v_ref, lam_ref, gsub_ref, o_ref, m_ref, l_ref, acc_ref,
                 bias_ref, decay_ref, *, lam_init):
    qi = pl.program_id(2)
    low = _lane_is_low((ATT_BLOCK, LANES))

    def stacked(q):
        zero = jnp.zeros_like(q)
        return jnp.concatenate([jnp.where(low, q, zero), jnp.where(low, zero, q)], axis=0)

    qq = [stacked(q_ref[0, g]) for g in range(ATT_GROUPS)]

    @pl.when(qi == 0)
    def _():
        r1, c1 = _causal_iotas(PAIR_ROWS, ATT_BLOCK)
        rel = (c1 - r1).astype(F32)
        for g in range(ATT_GROUPS):
            slope = slope_ref[pl.program_id(1) * ATT_GROUPS + g] * LOG2E
            bias_ref[g * PAIR_ROWS:(g + 1) * PAIR_ROWS, :] = slope * rel
            decay_ref[g * PAIR_ROWS:(g + 1) * PAIR_ROWS, :] = jnp.full((PAIR_ROWS, LANES), slope * ATT_BLOCK, F32)

    def block(kb):
        return pl.ds(pl.multiple_of(kb * ATT_BLOCK, ATT_BLOCK), ATT_BLOCK)

    def scores(kb):
        ks = block(kb)
        s = jnp.concatenate([_dot_nt(qq[g], k_ref[0, g, ks, :]) for g in range(ATT_GROUPS)], axis=0)
        return s + bias_ref[...]

    def weighted_values(kb, p):
        ks = block(kb)
        return jnp.concatenate([_dot(p[g * PAIR_ROWS:(g + 1) * PAIR_ROWS], v_ref[0, g, ks, :])
                                for g in range(ATT_GROUPS)], axis=0)

    r, c = _causal_iotas(ATT_GROUPS * PAIR_ROWS, ATT_BLOCK)
    o = _softmax_walk(qi, scores, weighted_values, c <= r, decay_ref[...], m_ref, l_ref, acc_ref)

    lp = lam_ref[...]
    lam = (jnp.exp(jnp.sum(lp[0:1] * lp[1:2], axis=-1, keepdims=True))
           - jnp.exp(jnp.sum(lp[2:3] * lp[3:4], axis=-1, keepdims=True)) + lam_init)
    outs = []
    for g in range(ATT_GROUPS):
        og = o[g * PAIR_ROWS:g * PAIR_ROWS + ATT_BLOCK] - lam * o[g * PAIR_ROWS + ATT_BLOCK:(g + 1) * PAIR_ROWS]
        outs.append((_rms(og, gsub_ref[...]) * (1.0 - lam_init)).astype(BF16))
    o_ref[0] = jnp.concatenate(outs, axis=1)


def _diff(q, k, v, slopes, lam_p, gsub, lam_init):
    b, _, s, _ = q.shape
    qs, ks, vs, os_ = _att_specs(1, s)
    rows = ATT_GROUPS * PAIR_ROWS
    return pl.pallas_call(
        functools.partial(_diff_kernel, lam_init=lam_init),
        grid=(b, DIFF_HEADS // ATT_GROUPS, s // ATT_BLOCK),
        in_specs=[pl.BlockSpec(memory_space=pltpu.SMEM), qs, ks, vs,
                  pl.BlockSpec(lam_p.shape, lambda b_, g, i: (0, 0)),
                  pl.BlockSpec(gsub.shape, lambda b_, g, i: (0, 0))],
        out_specs=os_,
        out_shape=jax.ShapeDtypeStruct((b, s, DIFF_HEADS * LANES), BF16),
        scratch_shapes=_att_scratch() + [pltpu.VMEM((rows, ATT_BLOCK), F32), pltpu.VMEM((rows, LANES), F32)],
        compiler_params=_params(("parallel", "parallel", "arbitrary")),
        name="diff",
    )(slopes, q, k, v, lam_p, gsub)


def _sb_kernel(q_ref, k_ref, v_ref, o_ref, run_ref, acc_ref):
    qi = pl.program_id(2)
    low = _lane_is_low((SB_BLOCK, LANES))
    pair_rows = 2 * SB_BLOCK

    def stacked(q):
        zero = jnp.zeros_like(q)
        return jnp.concatenate([jnp.where(low, q, zero), jnp.where(low, zero, q)], axis=0)

    qq = [stacked(q_ref[0, p]) for p in range(SB_PAIRS)]
    r, c = _causal_iotas(SB_PAIRS * pair_rows, SB_BLOCK)
    strict = c < r
    kr = lax.broadcasted_iota(jnp.int32, (SB_BLOCK, SB_BLOCK), 0)
    kc = lax.broadcasted_iota(jnp.int32, (SB_BLOCK, SB_BLOCK), 1)
    later = (kr > kc).astype(BF16)
    run_ref[...] = jnp.zeros(run_ref.shape, F32)
    acc_ref[...] = jnp.zeros(acc_ref.shape, F32)

    def step(kb, diagonal):
        ks = pl.ds(pl.multiple_of(kb * SB_BLOCK, SB_BLOCK), SB_BLOCK)
        s = jnp.concatenate([_dot_nt(qq[p], k_ref[0, p, ks, :]) for p in range(SB_PAIRS)], axis=0)
        soft = jnp.log(1.0 + jnp.exp(-jnp.abs(s)))
        log_beta = jnp.minimum(s, 0.0) - soft
        log_1m = log_beta - s
        if diagonal:
            log_1m = jnp.where(strict, log_1m, 0.0)
        hi = log_1m.astype(BF16)
        lo = (log_1m - hi.astype(F32)).astype(BF16)
        suffix = _dot(hi, later) + _dot(lo, later)
        run = run_ref[...]
        w = jnp.exp(log_beta + suffix + jnp.concatenate([run] * (SB_BLOCK // LANES), axis=1))
        if diagonal:
            w = jnp.where(strict, w, 0.0)
        w = w.astype(BF16)
        acc_ref[...] += jnp.concatenate(
            [_dot(w[p * pair_rows:(p + 1) * pair_rows], v_ref[0, p, ks, :]) for p in range(SB_PAIRS)], axis=0)
        run = run + jnp.sum(log_1m, axis=-1, keepdims=True)
        run_ref[...] = run
        return jnp.max(run)

    top = step(qi, True)

    def cond(carry):
        kb, top = carry
        return jnp.logical_and(kb >= 0, top > SB_DEAD)

    def body(carry):
        kb, _ = carry
        return kb - 1, step(kb, False)

    lax.while_loop(cond, body, (qi - 1, top))

    o = acc_ref[...]
    o_ref[0] = jnp.concatenate(
        [jnp.where(low, o[p * pair_rows:p * pair_rows + SB_BLOCK], o[p * pair_rows + SB_BLOCK:(p + 1) * pair_rows])
         for p in range(SB_PAIRS)], axis=1).astype(BF16)


def _sb(q, k, v):
    b, pairs, s, _ = q.shape
    rows = SB_PAIRS * 2 * SB_BLOCK
    kv = pl.BlockSpec((1, SB_PAIRS, s, LANES), lambda b_, g, i: (b_, g, 0, 0))
    return pl.pallas_call(
        _sb_kernel,
        grid=(b, pairs // SB_PAIRS, s // SB_BLOCK),
        in_specs=[pl.BlockSpec((1, SB_PAIRS, SB_BLOCK, LANES), lambda b_, g, i: (b_, g, i, 0)), kv, kv],
        out_specs=pl.BlockSpec((1, SB_BLOCK, SB_PAIRS * LANES), lambda b_, g, i: (b_, i, g)),
        out_shape=jax.ShapeDtypeStruct((b, s, pairs * LANES), BF16),
        scratch_shapes=[pltpu.VMEM((rows, LANES), F32), pltpu.VMEM((rows, LANES), F32)],
        compiler_params=_params(("parallel", "parallel", "arbitrary")),
        name="sb",
    )(q, k, v)


def _rope_tables(seq):
    half = MLA_ROPE // 2
    freqs = 1.0 / (ROPE_THETA ** (jnp.arange(half, dtype=F32) / half))
    ang = jnp.arange(seq, dtype=F32)[:, None] * freqs[None, :]
    cos, sin = jnp.cos(ang), jnp.sin(ang)
    one = jnp.ones((seq, MLA_NOPE), F32)
    zn = jnp.zeros((seq, MLA_NOPE), F32)
    zh = jnp.zeros((seq, half), F32)
    zt = jnp.zeros((seq, LANES - MLA_NOPE - MLA_ROPE), F32)
    cos_t = jnp.concatenate([one, cos, cos, zt], axis=1)
    sin_a = jnp.concatenate([zn, -sin, zh, zt], axis=1)
    sin_b = jnp.concatenate([zn, zh, sin, zt], axis=1)
    return cos_t, sin_a, sin_b


def _layer_weights(w_in, w_uq, w_ukv):
    pad = LANES - MLA_NOPE - MLA_ROPE
    wc = w_in[:, :MLA_Q_LORA + MLA_KV_LORA]
    wkr = jnp.pad(w_in[:, MLA_Q_LORA + MLA_KV_LORA:MLA_IN], ((0, 0), (MLA_NOPE, pad)))
    wd = w_in[:, MLA_IN:MLA_IN + DIFF_IN]
    ws = w_in[:, MLA_IN + DIFF_IN:]
    wuq = jnp.pad(w_uq.reshape(MLA_Q_LORA, MLA_HEADS, MLA_NOPE + MLA_ROPE),
                  ((0, 0), (0, 0), (0, pad))).reshape(MLA_Q_LORA, MLA_HEADS * LANES)
    kv = w_ukv.reshape(MLA_KV_LORA, MLA_HEADS, MLA_NOPE + MLA_V)
    wuk = jnp.pad(kv[:, :, :MLA_NOPE], ((0, 0), (0, 0), (0, LANES - MLA_NOPE))).reshape(
        MLA_KV_LORA, MLA_HEADS * LANES)
    wuv = kv[:, :, MLA_NOPE:].reshape(MLA_KV_LORA, MLA_HEADS * MLA_V)
    return [w.astype(BF16) for w in (wc, wkr, wd, ws, wuq, wuk, wuv)]


def kernel(x, p, g_ffn1, w1_a, w3_a, w2_a, g_mix, w_in, g_cq, g_ckv, w_uq, w_ukv, lambda_q1, lambda_k1, lambda_q2, lambda_k2, g_subln, w_o_mla, w_o_diff, w_o_sb, w_branch_gate, w_out, g_ffn2, w1_b, w3_b, w2_b, g_ple, w_ple_gate, w_ple_proj, g_final):
    b, s, d = x.shape
    t = b * s
    assert d == D_MODEL and t % ROW_TILE == 0 and s % PROJ_TILE == 0 and s % ATT_BLOCK == 0
    cos_t, sin_a, sin_b = _rope_tables(s)
    slopes = jnp.asarray(np.exp2(-8.0 * np.arange(1, DIFF_HEADS + 1, dtype=np.float32) / DIFF_HEADS), F32)
    row = lambda v: v.reshape(1, -1)
    w1_a, w3_a, w2_a, w1_b, w3_b, w2_b = map(_to_bf16, (w1_a, w3_a, w2_a, w1_b, w3_b, w2_b))
    w_in, w_uq, w_ukv = map(_to_bf16, (w_in, w_uq, w_ukv))
    w_branch_gate, w_o_mla, w_o_diff, w_o_sb, w_out = map(
        _to_bf16, (w_branch_gate, w_o_mla, w_o_diff, w_o_sb, w_out))
    w_ple_gate, w_ple_proj = map(_to_bf16, (w_ple_gate, w_ple_proj))

    h = x.reshape(t, d)
    for i in range(DEPTH):
        lam_init = 0.8 - 0.6 * math.exp(-0.3 * i)
        h = _ffn(h, row(g_ffn1[i]), w1_a, w3_a, w2_a, i)
        wc, wkr, wd, ws, wuq, wuk, wuv = _layer_weights(w_in[i], w_uq[i], w_ukv[i])
        qm, km, vm, qd, kd, vd, qs, ks, vs = _proj(
            h.reshape(b, s, d), row(g_mix[i]), wc, wkr, wd, ws, row(g_cq[i]), row(g_ckv[i]),
            wuq, wuk, wuv, cos_t, sin_a, sin_b)
        y_mla = _mla(qm, km, vm).reshape(t, -1)
        lam_p = jnp.stack([lambda_q1[i], lambda_k1[i], lambda_q2[i], lambda_k2[i]])
        y_diff = _diff(qd, kd, vd, slopes, lam_p, row(g_subln[i]), lam_init).reshape(t, -1)
        y_sb = _sb(qs, ks, vs).reshape(t, -1)
        h = _merge(h, row(g_mix[i]), y_mla, y_diff, y_sb, w_branch_gate, w_o_mla, w_o_diff, w_o_sb, w_out, i)
        h = _ffn_ple(h, row(g_ffn2[i]), w1_b, w3_b, w2_b, row(g_ple[i]), p[i].reshape(t, PLE_DIM),
                     w_ple_gate, w_ple_proj, row(g_final), i, final_norm=(i == DEPTH - 1))
    return h.reshape(b, s, d)
```

```python
import functools
import math

import jax
import jax.numpy as jnp
import numpy as np
from jax import lax
from jax.experimental import pallas as pl
from jax.experimental.pallas import tpu as pltpu

D_MODEL = 1024
DEPTH = 2
PLE_DIM = 256
EPS = 1e-6
ROPE_THETA = 10000.0

MLA_HEADS = 8
MLA_Q_LORA = 384
MLA_KV_LORA = 256
MLA_NOPE = 64
MLA_ROPE = 32
MLA_V = 64
MLA_IN = MLA_Q_LORA + MLA_KV_LORA + MLA_ROPE

DIFF_HEADS = 4
DIFF_QK = 64
DIFF_V = 2 * DIFF_QK
DIFF_QK_W = DIFF_HEADS * 2 * DIFF_QK
DIFF_IN = 2 * DIFF_QK_W + DIFF_HEADS * DIFF_V

SB_HEADS = 8
SB_DIM = 64
SB_W = SB_HEADS * SB_DIM
SB_IN = 3 * SB_W

N_BRANCH = 3
D_FF = 2816

LANES = 128
HALF = LANES // 2
VMEM_LIMIT = 56 * 1024 * 1024
CAST_BLOCK_BYTES = 4 * 1024 * 1024
BF16_SUBLANES = 16
NEG = -1e30
SB_DEAD = -104.0

ROW_TILE = 512
PROJ_TILE = 512
ATT_BLOCK = 512
ATT_GROUPS = 4
PAIR_ROWS = 2 * ATT_BLOCK
SB_BLOCK = 256
SB_PAIRS = 4
LOG2E = math.log2(math.e)
MLA_Q_SCALE = (MLA_NOPE + MLA_ROPE) ** -0.5 * LOG2E
DIFF_Q_SCALE = DIFF_QK ** -0.5 * LOG2E

BF16 = jnp.bfloat16
F32 = jnp.float32


def _dot(a, b):
    return jnp.dot(a, b, preferred_element_type=F32)


def _dot_nt(a, b):
    return lax.dot_general(a, b, (((1,), (1,)), ((), ())), preferred_element_type=F32)


def _rms(x, g):
    ms = jnp.mean(x * x, axis=-1, keepdims=True)
    return x * lax.rsqrt(ms + EPS) * g


def _sigmoid(x):
    return 1.0 / (1.0 + jnp.exp(-x))


def _resident(shape):
    nd = len(shape)
    return pl.BlockSpec(shape, lambda *_: (0,) * nd, pipeline_mode=pl.Buffered(1))


def _layer(w, layer):
    return pl.BlockSpec((None,) + w.shape[1:], lambda *_: (layer, 0, 0), pipeline_mode=pl.Buffered(1))


def _params(sem):
    return pltpu.CompilerParams(dimension_semantics=sem, vmem_limit_bytes=VMEM_LIMIT)


def _cast_kernel(x_ref, o_ref):
    o_ref[...] = x_ref[...].astype(BF16)


def _to_bf16(w):
    x = w.reshape(-1, w.shape[-1])
    rows, n = x.shape
    tile = rows
    while tile * n * 4 > CAST_BLOCK_BYTES and tile % (2 * BF16_SUBLANES) == 0:
        tile //= 2
    blk = pl.BlockSpec((tile, n), lambda i: (i, 0))
    out = pl.pallas_call(
        _cast_kernel,
        grid=(rows // tile,),
        in_specs=[blk],
        out_specs=blk,
        out_shape=jax.ShapeDtypeStruct(x.shape, BF16),
        compiler_params=_params(("parallel",)),
        name="cast",
    )(x)
    return out.reshape(w.shape)


def _half_ffn(x, g_ref, w1_ref, w3_ref, w2_ref):
    n = _rms(x, g_ref[...]).astype(BF16)
    a = _dot(n, w1_ref[...])
    b = _dot(n, w3_ref[...])
    act = (a * _sigmoid(a) * b).astype(BF16)
    return x + 0.5 * _dot(act, w2_ref[...])


def _ffn_kernel(h_ref, g_ref, w1_ref, w3_ref, w2_ref, o_ref):
    o_ref[...] = _half_ffn(h_ref[...], g_ref, w1_ref, w3_ref, w2_ref)


def _ffn_ple_kernel(h_ref, g_ref, w1_ref, w3_ref, w2_ref, gp_ref, p_ref, wpg_ref, wpe_ref, gfin_ref,
                    o_ref, *, final_norm):
    y = _half_ffn(h_ref[...], g_ref, w1_ref, w3_ref, w2_ref)
    gate = _sigmoid(_dot(_rms(y, gp_ref[...]).astype(BF16), wpg_ref[...]))
    y = y + gate * _dot(p_ref[...].astype(BF16), wpe_ref[...])
    if final_norm:
        y = _rms(y, gfin_ref[...])
    o_ref[...] = y


def _ffn_ple(h, g, w1, w3, w2, gp, p, wpg, wpe, gfin, layer, final_norm):
    t = h.shape[0]
    row = pl.BlockSpec((ROW_TILE, D_MODEL), lambda i: (i, 0))
    return pl.pallas_call(
        functools.partial(_ffn_ple_kernel, final_norm=final_norm),
        grid=(t // ROW_TILE,),
        in_specs=[row, _resident(g.shape), _layer(w1, layer), _layer(w3, layer), _layer(w2, layer),
                  _resident(gp.shape),
                  pl.BlockSpec((ROW_TILE, PLE_DIM), lambda i: (layer * (t // ROW_TILE) + i, 0)),
                  _layer(wpg, layer), _layer(wpe, layer), _resident(gfin.shape)],
        out_specs=row,
        out_shape=jax.ShapeDtypeStruct(h.shape, F32),
        compiler_params=_params(("parallel",)),
        name="ffn_ple",
    )(h, g, w1, w3, w2, gp, p, wpg, wpe, gfin)


def _ffn(h, g, w1, w3, w2, layer):
    t = h.shape[0]
    row = pl.BlockSpec((ROW_TILE, D_MODEL), lambda i: (i, 0))
    return pl.pallas_call(
        _ffn_kernel,
        grid=(t // ROW_TILE,),
        in_specs=[row, _resident((1, D_MODEL)), _layer(w1, layer), _layer(w3, layer), _layer(w2, layer)],
        out_specs=row,
        out_shape=jax.ShapeDtypeStruct(h.shape, F32),
        compiler_params=_params(("parallel",)),
        name="ffn",
    )(h, g, w1, w3, w2)


def _rope(x, cos, sin_a, sin_b):
    return (x * cos + pltpu.roll(x, LANES - MLA_ROPE // 2, 1) * sin_a
            + pltpu.roll(x, MLA_ROPE // 2, 1) * sin_b)


def _proj_kernel(h_ref, g_ref, wc_ref, wkr_ref, wd_ref, ws_ref, gcq_ref, gckv_ref, wuq_ref,
                 wuk_ref, wuv_ref, cos_ref, sa_ref, sb_ref,
                 qm_ref, km_ref, vm_ref, qd_ref, kd_ref, vd_ref, qs_ref, ks_ref, vs_ref):
    u = _rms(h_ref[0], g_ref[...]).astype(BF16)
    cos, sin_a, sin_b = cos_ref[...], sa_ref[...], sb_ref[...]

    c = _dot(u, wc_ref[...])
    cq = _rms(c[:, :MLA_Q_LORA], gcq_ref[...]).astype(BF16)
    ckv = _rms(c[:, MLA_Q_LORA:], gckv_ref[...]).astype(BF16)
    q = _dot(cq, wuq_ref[...])
    kn = _dot(ckv, wuk_ref[...])
    v = _dot(ckv, wuv_ref[...])
    kr = _rope(_dot(u, wkr_ref[...]), cos, sin_a, sin_b)
    for hd in range(MLA_HEADS):
        sl = slice(hd * LANES, (hd + 1) * LANES)
        qm_ref[0, hd] = (_rope(q[:, sl], cos, sin_a, sin_b) * MLA_Q_SCALE).astype(BF16)
        km_ref[0, hd] = (kn[:, sl] + kr).astype(BF16)
    for p in range(MLA_HEADS // 2):
        vm_ref[0, p] = v[:, p * LANES:(p + 1) * LANES].astype(BF16)

    d = _dot(u, wd_ref[...])
    for hd in range(DIFF_HEADS):
        sl = slice(hd * LANES, (hd + 1) * LANES)
        qd_ref[0, hd] = (d[:, sl] * DIFF_Q_SCALE).astype(BF16)
        kd_ref[0, hd] = d[:, DIFF_QK_W + hd * LANES:DIFF_QK_W + (hd + 1) * LANES].astype(BF16)
        vd_ref[0, hd] = d[:, 2 * DIFF_QK_W + hd * LANES:2 * DIFF_QK_W + (hd + 1) * LANES].astype(BF16)

    s = _dot(u, ws_ref[...])
    for p in range(SB_HEADS // 2):
        sl = slice(p * LANES, (p + 1) * LANES)
        qs_ref[0, p] = (s[:, sl] * SB_DIM ** -0.5).astype(BF16)
        ks_ref[0, p] = s[:, SB_W + p * LANES:SB_W + (p + 1) * LANES].astype(BF16)
        vs_ref[0, p] = s[:, 2 * SB_W + p * LANES:2 * SB_W + (p + 1) * LANES].astype(BF16)


def _proj(h3, g, wc, wkr, wd, ws, gcq, gckv, wuq, wuk, wuv, cos, sin_a, sin_b):
    b, s, _ = h3.shape
    tab = pl.BlockSpec((PROJ_TILE, LANES), lambda bi, si: (si, 0))

    def heads(n):
        return pl.BlockSpec((1, n, PROJ_TILE, LANES), lambda bi, si: (bi, 0, si, 0))

    def out(n):
        return jax.ShapeDtypeStruct((b, n, s, LANES), BF16)

    n_out = [MLA_HEADS, MLA_HEADS, MLA_HEADS // 2, DIFF_HEADS, DIFF_HEADS, DIFF_HEADS,
             SB_HEADS // 2, SB_HEADS // 2, SB_HEADS // 2]
    return pl.pallas_call(
        _proj_kernel,
        grid=(b, s // PROJ_TILE),
        in_specs=[pl.BlockSpec((1, PROJ_TILE, D_MODEL), lambda bi, si: (bi, si, 0)),
                  _resident(g.shape), _resident(wc.shape), _resident(wkr.shape),
                  _resident(wd.shape), _resident(ws.shape), _resident(gcq.shape),
                  _resident(gckv.shape), _resident(wuq.shape), _resident(wuk.shape),
                  _resident(wuv.shape), tab, tab, tab],
        out_specs=[heads(n) for n in n_out],
        out_shape=[out(n) for n in n_out],
        compiler_params=_params(("parallel", "parallel")),
        name="proj",
    )(h3, g, wc, wkr, wd, ws, gcq, gckv, wuq, wuk, wuv, cos, sin_a, sin_b)


def _merge_kernel(h_ref, g_ref, ya_ref, yb_ref, yc_ref, wg_ref, woa_ref, wob_ref, woc_ref,
                  wout_ref, o_ref):
    x = h_ref[...]
    u = _rms(x, g_ref[...]).astype(BF16)
    merged = None
    for i, (y_ref, wo_ref) in enumerate(((ya_ref, woa_ref), (yb_ref, wob_ref), (yc_ref, woc_ref))):
        gate = _sigmoid(_dot(u, wg_ref[:, i * D_MODEL:(i + 1) * D_MODEL]))
        term = gate * _dot(y_ref[...], wo_ref[...])
        merged = term if merged is None else merged + term
    o_ref[...] = x + _dot(merged.astype(BF16), wout_ref[...])


def _merge(h, g, ya, yb, yc, wg, woa, wob, woc, wout, layer):
    t = h.shape[0]
    row = pl.BlockSpec((ROW_TILE, D_MODEL), lambda i: (i, 0))
    yrow = pl.BlockSpec((ROW_TILE, ya.shape[1]), lambda i: (i, 0))
    return pl.pallas_call(
        _merge_kernel,
        grid=(t // ROW_TILE,),
        in_specs=[row, _resident(g.shape), yrow, yrow, yrow, _layer(wg, layer), _layer(woa, layer),
                  _layer(wob, layer), _layer(woc, layer), _layer(wout, layer)],
        out_specs=row,
        out_shape=jax.ShapeDtypeStruct(h.shape, F32),
        compiler_params=_params(("parallel",)),
        name="merge",
    )(h, g, ya, yb, yc, wg, woa, wob, woc, wout)


def _causal_iotas(rows, block):
    r = jnp.bitwise_and(lax.broadcasted_iota(jnp.int32, (rows, block), 0), block - 1)
    c = lax.broadcasted_iota(jnp.int32, (rows, block), 1)
    return r, c


def _softmax_step(s, pv, decay, m_ref, l_ref, acc_ref):
    m_old = m_ref[...]
    if decay is not None:
        m_old = m_old - decay
    m_new = jnp.maximum(m_old, jnp.max(s, axis=-1, keepdims=True))
    alpha = jnp.exp2(m_old - m_new)
    ps = [jnp.exp2(s[:, j * LANES:(j + 1) * LANES] - m_new) for j in range(s.shape[1] // LANES)]
    l_ref[...] = alpha * l_ref[...] + functools.reduce(lambda a, b: a + b, ps)
    p = jnp.concatenate([x.astype(BF16) for x in ps], axis=1)
    acc_ref[...] = alpha * acc_ref[...] + pv(p)
    m_ref[...] = m_new


def _softmax_walk(qi, scores, weighted_values, mask, decay, m_ref, l_ref, acc_ref):
    _softmax_init(m_ref, l_ref, acc_ref)

    def body(kb, carry):
        _softmax_step(scores(kb), functools.partial(weighted_values, kb), decay, m_ref, l_ref, acc_ref)
        return carry

    lax.fori_loop(0, qi, body, 0)
    _softmax_step(jnp.where(mask, scores(qi), NEG), functools.partial(weighted_values, qi), decay,
                  m_ref, l_ref, acc_ref)
    return acc_ref[...] / jnp.sum(l_ref[...], axis=-1, keepdims=True)


def _softmax_init(m_ref, l_ref, acc_ref):
    m_ref[...] = jnp.full(m_ref.shape, NEG, F32)
    l_ref[...] = jnp.zeros(l_ref.shape, F32)
    acc_ref[...] = jnp.zeros(acc_ref.shape, F32)


def _lane_is_low(shape):
    return lax.broadcasted_iota(jnp.int32, shape, len(shape) - 1) < HALF


def _mla_kernel(q_ref, k_ref, v_ref, o_ref, m_ref, l_ref, acc_ref):
    qi = pl.program_id(2)
    heads = 2 * ATT_GROUPS
    qs = [q_ref[0, hd] for hd in range(heads)]

    def block(kb):
        return pl.ds(pl.multiple_of(kb * ATT_BLOCK, ATT_BLOCK), ATT_BLOCK)

    def scores(kb):
        ks = block(kb)
        return jnp.concatenate([_dot_nt(qs[hd], k_ref[0, hd, ks, :]) for hd in range(heads)], axis=0)

    def weighted_values(kb, p):
        ks = block(kb)
        return jnp.concatenate([_dot(p[g * PAIR_ROWS:(g + 1) * PAIR_ROWS], v_ref[0, g, ks, :])
                                for g in range(ATT_GROUPS)], axis=0)

    r, c = _causal_iotas(ATT_GROUPS * PAIR_ROWS, ATT_BLOCK)
    o = _softmax_walk(qi, scores, weighted_values, c <= r, None, m_ref, l_ref, acc_ref)
    low = _lane_is_low((ATT_BLOCK, LANES))
    o_ref[0] = jnp.concatenate(
        [jnp.where(low, o[g * PAIR_ROWS:g * PAIR_ROWS + ATT_BLOCK], o[g * PAIR_ROWS + ATT_BLOCK:(g + 1) * PAIR_ROWS])
         for g in range(ATT_GROUPS)], axis=1).astype(BF16)


def _att_specs(heads_per_group, seq):
    q = pl.BlockSpec((1, ATT_GROUPS * heads_per_group, ATT_BLOCK, LANES), lambda b, g, i: (b, g, i, 0))
    k = pl.BlockSpec((1, ATT_GROUPS * heads_per_group, seq, LANES), lambda b, g, i: (b, g, 0, 0))
    v = pl.BlockSpec((1, ATT_GROUPS, seq, LANES), lambda b, g, i: (b, g, 0, 0))
    o = pl.BlockSpec((1, ATT_BLOCK, ATT_GROUPS * LANES), lambda b, g, i: (b, i, g))
    return q, k, v, o


def _att_scratch():
    rows = ATT_GROUPS * PAIR_ROWS
    return [pltpu.VMEM((rows, LANES), F32), pltpu.VMEM((rows, LANES), F32), pltpu.VMEM((rows, LANES), F32)]


def _mla(q, k, v):
    b, _, s, _ = q.shape
    groups = MLA_HEADS // 2
    qs, ks, vs, os_ = _att_specs(2, s)
    return pl.pallas_call(
        _mla_kernel,
        grid=(b, groups // ATT_GROUPS, s // ATT_BLOCK),
        in_specs=[qs, ks, vs],
        out_specs=os_,
        out_shape=jax.ShapeDtypeStruct((b, s, groups * LANES), BF16),
        scratch_shapes=_att_scratch(),
        compiler_params=_params(("parallel", "parallel", "arbitrary")),
        name="mla",
    )(q, k, v)


def _diff_kernel(slope_ref, q_ref, k_ref, v_ref, lam_ref, gsub_ref, o_ref, m_ref, l_ref, acc_ref,
                 bias_ref, decay_ref, *, lam_init):
    qi = pl.program_id(2)
    low = _lane_is_low((ATT_BLOCK, LANES))

    def stacked(q):
        zero = jnp.zeros_like(q)
        return jnp.concatenate([jnp.where(low, q, zero), jnp.where(low, zero, q)], axis=0)

    qq = [stacked(q_ref[0, g]) for g in range(ATT_GROUPS)]

    @pl.when(qi == 0)
    def _():
        r1, c1 = _causal_iotas(PAIR_ROWS, ATT_BLOCK)
        rel = (c1 - r1).astype(F32)
        for g in range(ATT_GROUPS):
            slope = slope_ref[pl.program_id(1) * ATT_GROUPS + g] * LOG2E
            bias_ref[g * PAIR_ROWS:(g + 1) * PAIR_ROWS, :] = slope * rel
            decay_ref[g * PAIR_ROWS:(g + 1) * PAIR_ROWS, :] = jnp.full((PAIR_ROWS, LANES), slope * ATT_BLOCK, F32)

    def block(kb):
        return pl.ds(pl.multiple_of(kb * ATT_BLOCK, ATT_BLOCK), ATT_BLOCK)

    def scores(kb):
        ks = block(kb)
        s = jnp.concatenate([_dot_nt(qq[g], k_ref[0, g, ks, :]) for g in range(ATT_GROUPS)], axis=0)
        return s + bias_ref[...]

    def weighted_values(kb, p):
        ks = block(kb)
        return jnp.concatenate([_dot(p[g * PAIR_ROWS:(g + 1) * PAIR_ROWS], v_ref[0, g, ks, :])
                                for g in range(ATT_GROUPS)], axis=0)

    r, c = _causal_iotas(ATT_GROUPS * PAIR_ROWS, ATT_BLOCK)
    o = _softmax_walk(qi, scores, weighted_values, c <= r, decay_ref[...], m_ref, l_ref, acc_ref)

    lp = lam_ref[...]
    lam = (jnp.exp(jnp.sum(lp[0:1] * lp[1:2], axis=-1, keepdims=True))
           - jnp.exp(jnp.sum(lp[2:3] * lp[3:4], axis=-1, keepdims=True)) + lam_init)
    outs = []
    for g in range(ATT_GROUPS):
        og = o[g * PAIR_ROWS:g * PAIR_ROWS + ATT_BLOCK] - lam * o[g * PAIR_ROWS + ATT_BLOCK:(g + 1) * PAIR_ROWS]
        outs.append((_rms(og, gsub_ref[...]) * (1.0 - lam_init)).astype(BF16))
    o_ref[0] = jnp.concatenate(outs, axis=1)


def _diff(q, k, v, slopes, lam_p, gsub, lam_init):
    b, _, s, _ = q.shape
    qs, ks, vs, os_ = _att_specs(1, s)
    rows = ATT_GROUPS * PAIR_ROWS
    return pl.pallas_call(
        functools.partial(_diff_kernel, lam_init=lam_init),
        grid=(b, DIFF_HEADS // ATT_GROUPS, s // ATT_BLOCK),
        in_specs=[pl.BlockSpec(memory_space=pltpu.SMEM), qs, ks, vs,
                  pl.BlockSpec(lam_p.shape, lambda b_, g, i: (0, 0)),
                  pl.BlockSpec(gsub.shape, lambda b_, g, i: (0, 0))],
        out_specs=os_,
        out_shape=jax.ShapeDtypeStruct((b, s, DIFF_HEADS * LANES), BF16),
        scratch_shapes=_att_scratch() + [pltpu.VMEM((rows, ATT_BLOCK), F32), pltpu.VMEM((rows, LANES), F32)],
        compiler_params=_params(("parallel", "parallel", "arbitrary")),
        name="diff",
    )(slopes, q, k, v, lam_p, gsub)


def _sb_kernel(q_ref, k_ref, v_ref, o_ref, run_ref, acc_ref):
    qi = pl.program_id(2)
    low = _lane_is_low((SB_BLOCK, LANES))
    pair_rows = 2 * SB_BLOCK

    def stacked(q):
        zero = jnp.zeros_like(q)
        return jnp.concatenate([jnp.where(low, q, zero), jnp.where(low, zero, q)], axis=0)

    qq = [stacked(q_ref[0, p]) for p in range(SB_PAIRS)]
    r, c = _causal_iotas(SB_PAIRS * pair_rows, SB_BLOCK)
    strict = c < r
    kr = lax.broadcasted_iota(jnp.int32, (SB_BLOCK, SB_BLOCK), 0)
    kc = lax.broadcasted_iota(jnp.int32, (SB_BLOCK, SB_BLOCK), 1)
    later = (kr > kc).astype(BF16)
    run_ref[...] = jnp.zeros(run_ref.shape, F32)
    acc_ref[...] = jnp.zeros(acc_ref.shape, F32)

    def step(kb, diagonal):
        ks = pl.ds(pl.multiple_of(kb * SB_BLOCK, SB_BLOCK), SB_BLOCK)
        s = jnp.concatenate([_dot_nt(qq[p], k_ref[0, p, ks, :]) for p in range(SB_PAIRS)], axis=0)
        soft = jnp.log(1.0 + jnp.exp(-jnp.abs(s)))
        log_beta = jnp.minimum(s, 0.0) - soft
        log_1m = log_beta - s
        if diagonal:
            log_1m = jnp.where(strict, log_1m, 0.0)
        hi = log_1m.astype(BF16)
        lo = (log_1m - hi.astype(F32)).astype(BF16)
        suffix = _dot(hi, later) + _dot(lo, later)
        run = run_ref[...]
        w = jnp.exp(log_beta + suffix + jnp.concatenate([run] * (SB_BLOCK // LANES), axis=1))
        if diagonal:
            w = jnp.where(strict, w, 0.0)
        w = w.astype(BF16)
        acc_ref[...] += jnp.concatenate(
            [_dot(w[p * pair_rows:(p + 1) * pair_rows], v_ref[0, p, ks, :]) for p in range(SB_PAIRS)], axis=0)
        run = run + jnp.sum(log_1m, axis=-1, keepdims=True)
        run_ref[...] = run
        return jnp.max(run)

    top = step(qi, True)

    def cond(carry):
        kb, top = carry
        return jnp.logical_and(kb >= 0, top > SB_DEAD)

    def body(carry):
        kb, _ = carry
        return kb - 1, step(kb, False)

    lax.while_loop(cond, body, (qi - 1, top))

    o = acc_ref[...]
    o_ref[0] = jnp.concatenate(
        [jnp.where(low, o[p * pair_rows:p * pair_rows + SB_BLOCK], o[p * pair_rows + SB_BLOCK:(p + 1) * pair_rows])
         for p in range(SB_PAIRS)], axis=1).astype(BF16)


def _sb(q, k, v):
    b, pairs, s, _ = q.shape
    rows = SB_PAIRS * 2 * SB_BLOCK
    kv = pl.BlockSpec((1, SB_PAIRS, s, LANES), lambda b_, g, i: (b_, g, 0, 0))
    return pl.pallas_call(
        _sb_kernel,
        grid=(b, pairs // SB_PAIRS, s // SB_BLOCK),
        in_specs=[pl.BlockSpec((1, SB_PAIRS, SB_BLOCK, LANES), lambda b_, g, i: (b_, g, i, 0)), kv, kv],
        out_specs=pl.BlockSpec((1, SB_BLOCK, SB_PAIRS * LANES), lambda b_, g, i: (b_, i, g)),
        out_shape=jax.ShapeDtypeStruct((b, s, pairs * LANES), BF16),
        scratch_shapes=[pltpu.VMEM((rows, LANES), F32), pltpu.VMEM((rows, LANES), F32)],
        compiler_params=_params(("parallel", "parallel", "arbitrary")),
        name="sb",
    )(q, k, v)


def _rope_tables(seq):
    half = MLA_ROPE // 2
    freqs = 1.0 / (ROPE_THETA ** (jnp.arange(half, dtype=F32) / half))
    ang = jnp.arange(seq, dtype=F32)[:, None] * freqs[None, :]
    cos, sin = jnp.cos(ang), jnp.sin(ang)
    one = jnp.ones((seq, MLA_NOPE), F32)
    zn = jnp.zeros((seq, MLA_NOPE), F32)
    zh = jnp.zeros((seq, half), F32)
    zt = jnp.zeros((seq, LANES - MLA_NOPE - MLA_ROPE), F32)
    cos_t = jnp.concatenate([one, cos, cos, zt], axis=1)
    sin_a = jnp.concatenate([zn, -sin, zh, zt], axis=1)
    sin_b = jnp.concatenate([zn, zh, sin, zt], axis=1)
    return cos_t, sin_a, sin_b


def _layer_weights(w_in, w_uq, w_ukv):
    pad = LANES - MLA_NOPE - MLA_ROPE
    wc = w_in[:, :MLA_Q_LORA + MLA_KV_LORA]
    wkr = jnp.pad(w_in[:, MLA_Q_LORA + MLA_KV_LORA:MLA_IN], ((0, 0), (MLA_NOPE, pad)))
    wd = w_in[:, MLA_IN:MLA_IN + DIFF_IN]
    ws = w_in[:, MLA_IN + DIFF_IN:]
    wuq = jnp.pad(w_uq.reshape(MLA_Q_LORA, MLA_HEADS, MLA_NOPE + MLA_ROPE),
                  ((0, 0), (0, 0), (0, pad))).reshape(MLA_Q_LORA, MLA_HEADS * LANES)
    kv = w_ukv.reshape(MLA_KV_LORA, MLA_HEADS, MLA_NOPE + MLA_V)
    wuk = jnp.pad(kv[:, :, :MLA_NOPE], ((0, 0), (0, 0), (0, LANES - MLA_NOPE))).reshape(
        MLA_KV_LORA, MLA_HEADS * LANES)
    wuv = kv[:, :, MLA_NOPE:].reshape(MLA_KV_LORA, MLA_HEADS * MLA_V)
    return [w.astype(BF16) for w in (wc, wkr, wd, ws, wuq, wuk, wuv)]


def kernel(x, p, g_ffn1, w1_a, w3_a, w2_a, g_mix, w_in, g_cq, g_ckv, w_uq, w_ukv, lambda_q1, lambda_k1, lambda_q2, lambda_k2, g_subln, w_o_mla, w_o_diff, w_o_sb, w_branch_gate, w_out, g_ffn2, w1_b, w3_b, w2_b, g_ple, w_ple_gate, w_ple_proj, g_final):
    b, s, d = x.shape
    t = b * s
    assert d == D_MODEL and t % ROW_TILE == 0 and s % PROJ_TILE == 0 and s % ATT_BLOCK == 0
    cos_t, sin_a, sin_b = _rope_tables(s)
    slopes = jnp.asarray(np.exp2(-8.0 * np.arange(1, DIFF_HEADS + 1, dtype=np.float32) / DIFF_HEADS), F32)
    row = lambda v: v.reshape(1, -1)
    w1_a, w3_a, w2_a, w1_b, w3_b, w2_b = map(_to_bf16, (w1_a, w3_a, w2_a, w1_b, w3_b, w2_b))
    w_uq, w_ukv = map(_to_bf16, (w_uq, w_ukv))
    w_in = w_in.astype(BF16)
    w_branch_gate, w_o_mla, w_o_diff, w_o_sb, w_out = map(
        _to_bf16, (w_branch_gate, w_o_mla, w_o_diff, w_o_sb, w_out))
    w_ple_gate, w_ple_proj = map(_to_bf16, (w_ple_gate, w_ple_proj))

    h = x.reshape(t, d)
    for i in range(DEPTH):
        lam_init = 0.8 - 0.6 * math.exp(-0.3 * i)
        h = _ffn(h, row(g_ffn1[i]), w1_a, w3_a, w2_a, i)
        wc, wkr, wd, ws, wuq, wuk, wuv = _layer_weights(w_in[i], w_uq[i], w_ukv[i])
        qm, km, vm, qd, kd, vd, qs, ks, vs = _proj(
            h.reshape(b, s, d), row(g_mix[i]), wc, wkr, wd, ws, row(g_cq[i]), row(g_ckv[i]),
            wuq, wuk, wuv, cos_t, sin_a, sin_b)
        y_mla = _mla(qm, km, vm).reshape(t, -1)
        lam_p = jnp.stack([lambda_q1[i], lambda_k1[i], lambda_q2[i], lambda_k2[i]])
        y_diff = _diff(qd, kd, vd, slopes, lam_p, row(g_subln[i]), lam_init).reshape(t, -1)
        y_sb = _sb(qs, ks, vs).reshape(t, -1)
        h = _merge(h, row(g_mix[i]), y_mla, y_diff, y_sb, w_branch_gate, w_o_mla, w_o_diff, w_o_sb, w_out, i)
        h = _ffn_ple(h, row(g_ffn2[i]), w1_b, w3_b, w2_b, row(g_ple[i]), p.reshape(DEPTH * t, PLE_DIM),
                     w_ple_gate, w_ple_proj, row(g_final), i, final_norm=(i == DEPTH - 1))
    return h.reshape(b, s, d)
```

```python
import functools
import math

import jax
import jax.numpy as jnp
import numpy as np
from jax import lax
from jax.experimental import pallas as pl
from jax.experimental.pallas import tpu as pltpu

D_MODEL = 1024
DEPTH = 2
PLE_DIM = 256
EPS = 1e-6
ROPE_THETA = 10000.0

MLA_HEADS = 8
MLA_Q_LORA = 384
MLA_KV_LORA = 256
MLA_NOPE = 64
MLA_ROPE = 32
MLA_V = 64
MLA_IN = MLA_Q_LORA + MLA_KV_LORA + MLA_ROPE

DIFF_HEADS = 4
DIFF_QK = 64
DIFF_V = 2 * DIFF_QK
DIFF_QK_W = DIFF_HEADS * 2 * DIFF_QK
DIFF_IN = 2 * DIFF_QK_W + DIFF_HEADS * DIFF_V

SB_HEADS = 8
SB_DIM = 64
SB_W = SB_HEADS * SB_DIM
SB_IN = 3 * SB_W

N_BRANCH = 3
D_FF = 2816

LANES = 128
HALF = LANES // 2
VMEM_LIMIT = 56 * 1024 * 1024
CAST_BLOCK_BYTES = 4 * 1024 * 1024
BF16_SUBLANES = 16
NEG = -1e30
SB_DEAD = -104.0

ROW_TILE = 512
PROJ_TILE = 512
ATT_BLOCK = 512
ATT_GROUPS = 4
PAIR_ROWS = 2 * ATT_BLOCK
SB_BLOCK = 256
SB_PAIRS = 4
LOG2E = math.log2(math.e)
MLA_Q_SCALE = (MLA_NOPE + MLA_ROPE) ** -0.5 * LOG2E
DIFF_Q_SCALE = DIFF_QK ** -0.5 * LOG2E

BF16 = jnp.bfloat16
F32 = jnp.float32


def _dot(a, b):
    return jnp.dot(a, b, preferred_element_type=F32)


def _dot_nt(a, b):
    return lax.dot_general(a, b, (((1,), (1,)), ((), ())), preferred_element_type=F32)


def _rms(x, g):
    ms = jnp.mean(x * x, axis=-1, keepdims=True)
    return x * lax.rsqrt(ms + EPS) * g


def _sigmoid(x):
    return 1.0 / (1.0 + jnp.exp(-x))


def _resident(shape):
    nd = len(shape)
    return pl.BlockSpec(shape, lambda *_: (0,) * nd, pipeline_mode=pl.Buffered(1))


def _layer(w, layer):
    return pl.BlockSpec((None,) + w.shape[1:], lambda *_: (layer, 0, 0), pipeline_mode=pl.Buffered(1))


def _params(sem):
    return pltpu.CompilerParams(dimension_semantics=sem, vmem_limit_bytes=VMEM_LIMIT)


def _cast_kernel(x_ref, o_ref):
    o_ref[...] = x_ref[...].astype(BF16)


def _to_bf16(w):
    x = w.reshape(-1, w.shape[-1])
    rows, n = x.shape
    tile = rows
    while tile * n * 4 > CAST_BLOCK_BYTES and tile % (2 * BF16_SUBLANES) == 0:
        tile //= 2
    blk = pl.BlockSpec((tile, n), lambda i: (i, 0))
    out = pl.pallas_call(
        _cast_kernel,
        grid=(rows // tile,),
        in_specs=[blk],
        out_specs=blk,
        out_shape=jax.ShapeDtypeStruct(x.shape, BF16),
        compiler_params=_params(("parallel",)),
        name="cast",
    )(x)
    return out.reshape(w.shape)


def _half_ffn(x, g_ref, w1_ref, w3_ref, w2_ref):
    n = _rms(x, g_ref[...]).astype(BF16)
    a = _dot(n, w1_ref[...])
    b = _dot(n, w3_ref[...])
    act = (a * _sigmoid(a) * b).astype(BF16)
    return x + 0.5 * _dot(act, w2_ref[...])


def _ffn_kernel(h_ref, g_ref, w1_ref, w3_ref, w2_ref, o_ref):
    o_ref[...] = _half_ffn(h_ref[...], g_ref, w1_ref, w3_ref, w2_ref)


def _ffn_ple_kernel(h_ref, g_ref, w1_ref, w3_ref, w2_ref, gp_ref, p_ref, wpg_ref, wpe_ref, gfin_ref,
                    o_ref, *, final_norm):
    y = _half_ffn(h_ref[...], g_ref, w1_ref, w3_ref, w2_ref)
    gate = _sigmoid(_dot(_rms(y, gp_ref[...]).astype(BF16), wpg_ref[...]))
    y = y + gate * _dot(p_ref[...].astype(BF16), wpe_ref[...])
    if final_norm:
        y = _rms(y, gfin_ref[...])
    o_ref[...] = y


def _ffn_ple(h, g, w1, w3, w2, gp, p, wpg, wpe, gfin, layer, final_norm):
    t = h.shape[0]
    row = pl.BlockSpec((ROW_TILE, D_MODEL), lambda i: (i, 0))
    return pl.pallas_call(
        functools.partial(_ffn_ple_kernel, final_norm=final_norm),
        grid=(t // ROW_TILE,),
        in_specs=[row, _resident(g.shape), _layer(w1, layer), _layer(w3, layer), _layer(w2, layer),
                  _resident(gp.shape),
                  pl.BlockSpec((ROW_TILE, PLE_DIM), lambda i: (layer * (t // ROW_TILE) + i, 0)),
                  _layer(wpg, layer), _layer(wpe, layer), _resident(gfin.shape)],
        out_specs=row,
        out_shape=jax.ShapeDtypeStruct(h.shape, F32),
        compiler_params=_params(("parallel",)),
        name="ffn_ple",
    )(h, g, w1, w3, w2, gp, p, wpg, wpe, gfin)


def _ffn(h, g, w1, w3, w2, layer):
    t = h.shape[0]
    row = pl.BlockSpec((ROW_TILE, D_MODEL), lambda i: (i, 0))
    return pl.pallas_call(
        _ffn_kernel,
        grid=(t // ROW_TILE,),
        in_specs=[row, _resident((1, D_MODEL)), _layer(w1, layer), _layer(w3, layer), _layer(w2, layer)],
        out_specs=row,
        out_shape=jax.ShapeDtypeStruct(h.shape, F32),
        compiler_params=_params(("parallel",)),
        name="ffn",
    )(h, g, w1, w3, w2)


def _rope(x, cos, sin_a, sin_b):
    return (x * cos + pltpu.roll(x, LANES - MLA_ROPE // 2, 1) * sin_a
            + pltpu.roll(x, MLA_ROPE // 2, 1) * sin_b)


def _proj_kernel(h_ref, g_ref, wc_ref, wd_ref, ws_ref, gcq_ref, gckv_ref, wuq_ref,
                 wuk_ref, wuv_ref, cos_ref, sa_ref, sb_ref,
                 qm_ref, km_ref, vm_ref, qd_ref, kd_ref, vd_ref, qs_ref, ks_ref, vs_ref):
    u = _rms(h_ref[0], g_ref[...]).astype(BF16)
    cos, sin_a, sin_b = cos_ref[...], sa_ref[...], sb_ref[...]

    c = _dot(u, wc_ref[...])
    cq = _rms(c[:, :MLA_Q_LORA], gcq_ref[...]).astype(BF16)
    ckv = _rms(c[:, MLA_Q_LORA:MLA_Q_LORA + MLA_KV_LORA], gckv_ref[...]).astype(BF16)
    q = _dot(cq, wuq_ref[...])
    kn = _dot(ckv, wuk_ref[...])
    v = _dot(ckv, wuv_ref[...])
    kr = _rope(c[:, MLA_Q_LORA + MLA_KV_LORA:], cos, sin_a, sin_b)
    for hd in range(MLA_HEADS):
        sl = slice(hd * LANES, (hd + 1) * LANES)
        qm_ref[0, hd] = (_rope(q[:, sl], cos, sin_a, sin_b) * MLA_Q_SCALE).astype(BF16)
        km_ref[0, hd] = (kn[:, sl] + kr).astype(BF16)
    for p in range(MLA_HEADS // 2):
        vm_ref[0, p] = v[:, p * LANES:(p + 1) * LANES].astype(BF16)

    d = _dot(u, wd_ref[...])
    for hd in range(DIFF_HEADS):
        sl = slice(hd * LANES, (hd + 1) * LANES)
        qd_ref[0, hd] = (d[:, sl] * DIFF_Q_SCALE).astype(BF16)
        kd_ref[0, hd] = d[:, DIFF_QK_W + hd * LANES:DIFF_QK_W + (hd + 1) * LANES].astype(BF16)
        vd_ref[0, hd] = d[:, 2 * DIFF_QK_W + hd * LANES:2 * DIFF_QK_W + (hd + 1) * LANES].astype(BF16)

    s = _dot(u, ws_ref[...])
    for p in range(SB_HEADS // 2):
        sl = slice(p * LANES, (p + 1) * LANES)
        qs_ref[0, p] = (s[:, sl] * SB_DIM ** -0.5).astype(BF16)
        ks_ref[0, p] = s[:, SB_W + p * LANES:SB_W + (p + 1) * LANES].astype(BF16)
        vs_ref[0, p] = s[:, 2 * SB_W + p * LANES:2 * SB_W + (p + 1) * LANES].astype(BF16)


def _proj(h3, g, wc, wd, ws, gcq, gckv, wuq, wuk, wuv, cos, sin_a, sin_b):
    b, s, _ = h3.shape
    tab = pl.BlockSpec((PROJ_TILE, LANES), lambda bi, si: (si, 0))

    def heads(n):
        return pl.BlockSpec((1, n, PROJ_TILE, LANES), lambda bi, si: (bi, 0, si, 0))

    def out(n):
        return jax.ShapeDtypeStruct((b, n, s, LANES), BF16)

    n_out = [MLA_HEADS, MLA_HEADS, MLA_HEADS // 2, DIFF_HEADS, DIFF_HEADS, DIFF_HEADS,
             SB_HEADS // 2, SB_HEADS // 2, SB_HEADS // 2]
    return pl.pallas_call(
        _proj_kernel,
        grid=(b, s // PROJ_TILE),
        in_specs=[pl.BlockSpec((1, PROJ_TILE, D_MODEL), lambda bi, si: (bi, si, 0)),
                  _resident(g.shape), _resident(wc.shape),
                  _resident(wd.shape), _resident(ws.shape), _resident(gcq.shape),
                  _resident(gckv.shape), _resident(wuq.shape), _resident(wuk.shape),
                  _resident(wuv.shape), tab, tab, tab],
        out_specs=[heads(n) for n in n_out],
        out_shape=[out(n) for n in n_out],
        compiler_params=_params(("parallel", "parallel")),
        name="proj",
    )(h3, g, wc, wd, ws, gcq, gckv, wuq, wuk, wuv, cos, sin_a, sin_b)


def _merge_kernel(h_ref, g_ref, ya_ref, yb_ref, yc_ref, wg_ref, woa_ref, wob_ref, woc_ref,
                  wout_ref, o_ref):
    x = h_ref[...]
    u = _rms(x, g_ref[...]).astype(BF16)
    merged = None
    for i, (y_ref, wo_ref) in enumerate(((ya_ref, woa_ref), (yb_ref, wob_ref), (yc_ref, woc_ref))):
        gate = _sigmoid(_dot(u, wg_ref[:, i * D_MODEL:(i + 1) * D_MODEL]))
        term = gate * _dot(y_ref[...], wo_ref[...])
        merged = term if merged is None else merged + term
    o_ref[...] = x + _dot(merged.astype(BF16), wout_ref[...])


def _merge(h, g, ya, yb, yc, wg, woa, wob, woc, wout, layer):
    t = h.shape[0]
    row = pl.BlockSpec((ROW_TILE, D_MODEL), lambda i: (i, 0))
    yrow = pl.BlockSpec((ROW_TILE, ya.shape[1]), lambda i: (i, 0))
    return pl.pallas_call(
        _merge_kernel,
        grid=(t // ROW_TILE,),
        in_specs=[row, _resident(g.shape), yrow, yrow, yrow, _layer(wg, layer), _layer(woa, layer),
                  _layer(wob, layer), _layer(woc, layer), _layer(wout, layer)],
        out_specs=row,
        out_shape=jax.ShapeDtypeStruct(h.shape, F32),
        compiler_params=_params(("parallel",)),
        name="merge",
    )(h, g, ya, yb, yc, wg, woa, wob, woc, wout)


def _causal_iotas(rows, block):
    r = jnp.bitwise_and(lax.broadcasted_iota(jnp.int32, (rows, block), 0), block - 1)
    c = lax.broadcasted_iota(jnp.int32, (rows, block), 1)
    return r, c


def _softmax_step(s, pv, decay, m_ref, l_ref, acc_ref):
    m_old = m_ref[...]
    if decay is not None:
        m_old = m_old - decay
    m_new = jnp.maximum(m_old, jnp.max(s, axis=-1, keepdims=True))
    alpha = jnp.exp2(m_old - m_new)
    ps = [jnp.exp2(s[:, j * LANES:(j + 1) * LANES] - m_new) for j in range(s.shape[1] // LANES)]
    l_ref[...] = alpha * l_ref[...] + functools.reduce(lambda a, b: a + b, ps)
    p = jnp.concatenate([x.astype(BF16) for x in ps], axis=1)
    acc_ref[...] = alpha * acc_ref[...] + pv(p)
    m_ref[...] = m_new


def _softmax_walk(qi, scores, weighted_values, mask, decay, m_ref, l_ref, acc_ref):
    _softmax_init(m_ref, l_ref, acc_ref)

    def body(kb, carry):
        _softmax_step(scores(kb), functools.partial(weighted_values, kb), decay, m_ref, l_ref, acc_ref)
        return carry

    lax.fori_loop(0, qi, body, 0)
    _softmax_step(jnp.where(mask, scores(qi), NEG), functools.partial(weighted_values, qi), decay,
                  m_ref, l_ref, acc_ref)
    return acc_ref[...] / jnp.sum(l_ref[...], axis=-1, keepdims=True)


def _softmax_init(m_ref, l_ref, acc_ref):
    m_ref[...] = jnp.full(m_ref.shape, NEG, F32)
    l_ref[...] = jnp.zeros(l_ref.shape, F32)
    acc_ref[...] = jnp.zeros(acc_ref.shape, F32)


def _lane_is_low(shape):
    return lax.broadcasted_iota(jnp.int32, shape, len(shape) - 1) < HALF


def _mla_kernel(q_ref, k_ref, v_ref, o_ref, m_ref, l_ref, acc_ref):
    qi = pl.program_id(2)
    heads = 2 * ATT_GROUPS
    qs = [q_ref[0, hd] for hd in range(heads)]

    def block(kb):
        return pl.ds(pl.multiple_of(kb * ATT_BLOCK, ATT_BLOCK), ATT_BLOCK)

    def scores(kb):
        ks = block(kb)
        return jnp.concatenate([_dot_nt(qs[hd], k_ref[0, hd, ks, :]) for hd in range(heads)], axis=0)

    def weighted_values(kb, p):
        ks = block(kb)
        return jnp.concatenate([_dot(p[g * PAIR_ROWS:(g + 1) * PAIR_ROWS], v_ref[0, g, ks, :])
                                for g in range(ATT_GROUPS)], axis=0)

    r, c = _causal_iotas(ATT_GROUPS * PAIR_ROWS, ATT_BLOCK)
    o = _softmax_walk(qi, scores, weighted_values, c <= r, None, m_ref, l_ref, acc_ref)
    low = _lane_is_low((ATT_BLOCK, LANES))
    o_ref[0] = jnp.concatenate(
        [jnp.where(low, o[g * PAIR_ROWS:g * PAIR_ROWS + ATT_BLOCK], o[g * PAIR_ROWS + ATT_BLOCK:(g + 1) * PAIR_ROWS])
         for g in range(ATT_GROUPS)], axis=1).astype(BF16)


def _att_specs(heads_per_group, seq):
    q = pl.BlockSpec((1, ATT_GROUPS * heads_per_group, ATT_BLOCK, LANES), lambda b, g, i: (b, g, i, 0))
    k = pl.BlockSpec((1, ATT_GROUPS * heads_per_group, seq, LANES), lambda b, g, i: (b, g, 0, 0))
    v = pl.BlockSpec((1, ATT_GROUPS, seq, LANES), lambda b, g, i: (b, g, 0, 0))
    o = pl.BlockSpec((1, ATT_BLOCK, ATT_GROUPS * LANES), lambda b, g, i: (b, i, g))
    return q, k, v, o


def _att_scratch():
    rows = ATT_GROUPS * PAIR_ROWS
    return [pltpu.VMEM((rows, LANES), F32), pltpu.VMEM((rows, LANES), F32), pltpu.VMEM((rows, LANES), F32)]


def _mla(q, k, v):
    b, _, s, _ = q.shape
    groups = MLA_HEADS // 2
    qs, ks, vs, os_ = _att_specs(2, s)
    return pl.pallas_call(
        _mla_kernel,
        grid=(b, groups // ATT_GROUPS, s // ATT_BLOCK),
        in_specs=[qs, ks, vs],
        out_specs=os_,
        out_shape=jax.ShapeDtypeStruct((b, s, groups * LANES), BF16),
        scratch_shapes=_att_scratch(),
        compiler_params=_params(("parallel", "parallel", "arbitrary")),
        name="mla",
    )(q, k, v)


def _diff_kernel(slope_ref, q_ref, k_ref, v_ref, lam_ref, gsub_ref, o_ref, m_ref, l_ref, acc_ref,
                 bias_ref, decay_ref, *, lam_init):
    qi = pl.program_id(2)
    low = _lane_is_low((ATT_BLOCK, LANES))

    def stacked(q):
        zero = jnp.zeros_like(q)
        return jnp.concatenate([jnp.where(low, q, zero), jnp.where(low, zero, q)], axis=0)

    qq = [stacked(q_ref[0, g]) for g in range(ATT_GROUPS)]

    @pl.when(qi == 0)
    def _():
        r1, c1 = _causal_iotas(PAIR_ROWS, ATT_BLOCK)
        rel = (c1 - r1).astype(F32)
        for g in range(ATT_GROUPS):
            slope = slope_ref[pl.program_id(1) * ATT_GROUPS + g] * LOG2E
            bias_ref[g * PAIR_ROWS:(g + 1) * PAIR_ROWS, :] = slope * rel
            decay_ref[g * PAIR_ROWS:(g + 1) * PAIR_ROWS, :] = jnp.full((PAIR_ROWS, LANES), slope * ATT_BLOCK, F32)

    def block(kb):
        return pl.ds(pl.multiple_of(kb * ATT_BLOCK, ATT_BLOCK), ATT_BLOCK)

    def scores(kb):
        ks = block(kb)
        s = jnp.concatenate([_dot_nt(qq[g], k_ref[0, g, ks, :]) for g in range(ATT_GROUPS)], axis=0)
        return s + bias_ref[...]

    def weighted_values(kb, p):
        ks = block(kb)
        return jnp.concatenate([_dot(p[g * PAIR_ROWS:(g + 1) * PAIR_ROWS], v_ref[0, g, ks, :])
                                for g in range(ATT_GROUPS)], axis=0)

    r, c = _causal_iotas(ATT_GROUPS * PAIR_ROWS, ATT_BLOCK)
    o = _softmax_walk(qi, scores, weighted_values, c <= r, decay_ref[...], m_ref, l_ref, acc_ref)

    lp = lam_ref[...]
    lam = (jnp.exp(jnp.sum(lp[0:1] * lp[1:2], axis=-1, keepdims=True))
           - jnp.exp(jnp.sum(lp[2:3] * lp[3:4], axis=-1, keepdims=True)) + lam_init)
    outs = []
    for g in range(ATT_GROUPS):
        og = o[g * PAIR_ROWS:g * PAIR_ROWS + ATT_BLOCK] - lam * o[g * PAIR_ROWS + ATT_BLOCK:(g + 1) * PAIR_ROWS]
        outs.append((_rms(og, gsub_ref[...]) * (1.0 - lam_init)).astype(BF16))
    o_ref[0] = jnp.concatenate(outs, axis=1)


def _diff(q, k, v, slopes, lam_p, gsub, lam_init):
    b, _, s, _ = q.shape
    qs, ks, vs, os_ = _att_specs(1, s)
    rows = ATT_GROUPS * PAIR_ROWS
    return pl.pallas_call(
        functools.partial(_diff_kernel, lam_init=lam_init),
        grid=(b, DIFF_HEADS // ATT_GROUPS, s // ATT_BLOCK),
        in_specs=[pl.BlockSpec(memory_space=pltpu.SMEM), qs, ks, vs,
                  pl.BlockSpec(lam_p.shape, lambda b_, g, i: (0, 0)),
                  pl.BlockSpec(gsub.shape, lambda b_, g, i: (0, 0))],
        out_specs=os_,
        out_shape=jax.ShapeDtypeStruct((b, s, DIFF_HEADS * LANES), BF16),
        scratch_shapes=_att_scratch() + [pltpu.VMEM((rows, ATT_BLOCK), F32), pltpu.VMEM((rows, LANES), F32)],
        compiler_params=_params(("parallel", "parallel", "arbitrary")),
        name="diff",
    )(slopes, q, k, v, lam_p, gsub)


def _sb_kernel(q_ref, k_ref, v_ref, o_ref, run_ref, acc_ref):
    qi = pl.program_id(2)
    low = _lane_is_low((SB_BLOCK, LANES))
    pair_rows = 2 * SB_BLOCK

    def stacked(q):
        zero = jnp.zeros_like(q)
        return jnp.concatenate([jnp.where(low, q, zero), jnp.where(low, zero, q)], axis=0)

    qq = [stacked(q_ref[0, p]) for p in range(SB_PAIRS)]
    r, c = _causal_iotas(SB_PAIRS * pair_rows, SB_BLOCK)
    strict = c < r
    kr = lax.broadcasted_iota(jnp.int32, (SB_BLOCK, SB_BLOCK), 0)
    kc = lax.broadcasted_iota(jnp.int32, (SB_BLOCK, SB_BLOCK), 1)
    later = (kr > kc).astype(BF16)
    run_ref[...] = jnp.zeros(run_ref.shape, F32)
    acc_ref[...] = jnp.zeros(acc_ref.shape, F32)

    def step(kb, diagonal):
        ks = pl.ds(pl.multiple_of(kb * SB_BLOCK, SB_BLOCK), SB_BLOCK)
        s = jnp.concatenate([_dot_nt(qq[p], k_ref[0, p, ks, :]) for p in range(SB_PAIRS)], axis=0)
        soft = jnp.log(1.0 + jnp.exp(-jnp.abs(s)))
        log_beta = jnp.minimum(s, 0.0) - soft
        log_1m = log_beta - s
        if diagonal:
            log_1m = jnp.where(strict, log_1m, 0.0)
        hi = log_1m.astype(BF16)
        lo = (log_1m - hi.astype(F32)).astype(BF16)
        suffix = _dot(hi, later) + _dot(lo, later)
        run = run_ref[...]
        w = jnp.exp(log_beta + suffix + jnp.concatenate([run] * (SB_BLOCK // LANES), axis=1))
        if diagonal:
            w = jnp.where(strict, w, 0.0)
        w = w.astype(BF16)
        acc_ref[...] += jnp.concatenate(
            [_dot(w[p * pair_rows:(p + 1) * pair_rows], v_ref[0, p, ks, :]) for p in range(SB_PAIRS)], axis=0)
        run = run + jnp.sum(log_1m, axis=-1, keepdims=True)
        run_ref[...] = run
        return jnp.max(run)

    top = step(qi, True)

    def cond(carry):
        kb, top = carry
        return jnp.logical_and(kb >= 0, top > SB_DEAD)

    def body(carry):
        kb, _ = carry
        return kb - 1, step(kb, False)

    lax.while_loop(cond, body, (qi - 1, top))

    o = acc_ref[...]
    o_ref[0] = jnp.concatenate(
        [jnp.where(low, o[p * pair_rows:p * pair_rows + SB_BLOCK], o[p * pair_rows + SB_BLOCK:(p + 1) * pair_rows])
         for p in range(SB_PAIRS)], axis=1).astype(BF16)


def _sb(q, k, v):
    b, pairs, s, _ = q.shape
    rows = SB_PAIRS * 2 * SB_BLOCK
    kv = pl.BlockSpec((1, SB_PAIRS, s, LANES), lambda b_, g, i: (b_, g, 0, 0))
    return pl.pallas_call(
        _sb_kernel,
        grid=(b, pairs // SB_PAIRS, s // SB_BLOCK),
        in_specs=[pl.BlockSpec((1, SB_PAIRS, SB_BLOCK, LANES), lambda b_, g, i: (b_, g, i, 0)), kv, kv],
        out_specs=pl.BlockSpec((1, SB_BLOCK, SB_PAIRS * LANES), lambda b_, g, i: (b_, i, g)),
        out_shape=jax.ShapeDtypeStruct((b, s, pairs * LANES), BF16),
        scratch_shapes=[pltpu.VMEM((rows, LANES), F32), pltpu.VMEM((rows, LANES), F32)],
        compiler_params=_params(("parallel", "parallel", "arbitrary")),
        name="sb",
    )(q, k, v)


def _rope_tables(seq):
    half = MLA_ROPE // 2
    freqs = 1.0 / (ROPE_THETA ** (jnp.arange(half, dtype=F32) / half))
    ang = jnp.arange(seq, dtype=F32)[:, None] * freqs[None, :]
    cos, sin = jnp.cos(ang), jnp.sin(ang)
    one = jnp.ones((seq, MLA_NOPE), F32)
    zn = jnp.zeros((seq, MLA_NOPE), F32)
    zh = jnp.zeros((seq, half), F32)
    zt = jnp.zeros((seq, LANES - MLA_NOPE - MLA_ROPE), F32)
    cos_t = jnp.concatenate([one, cos, cos, zt], axis=1)
    sin_a = jnp.concatenate([zn, -sin, zh, zt], axis=1)
    sin_b = jnp.concatenate([zn, zh, sin, zt], axis=1)
    return cos_t, sin_a, sin_b


def _layer_weights(w_in, w_uq, w_ukv):
    pad = LANES - MLA_NOPE - MLA_ROPE
    wkr = jnp.pad(w_in[:, MLA_Q_LORA + MLA_KV_LORA:MLA_IN], ((0, 0), (MLA_NOPE, pad)))
    wc = jnp.concatenate([w_in[:, :MLA_Q_LORA + MLA_KV_LORA], wkr], axis=1)
    wd = w_in[:, MLA_IN:MLA_IN + DIFF_IN]
    ws = w_in[:, MLA_IN + DIFF_IN:]
    wuq = jnp.pad(w_uq.reshape(MLA_Q_LORA, MLA_HEADS, MLA_NOPE + MLA_ROPE),
                  ((0, 0), (0, 0), (0, pad))).reshape(MLA_Q_LORA, MLA_HEADS * LANES)
    kv = w_ukv.reshape(MLA_KV_LORA, MLA_HEADS, MLA_NOPE + MLA_V)
    wuk = jnp.pad(kv[:, :, :MLA_NOPE], ((0, 0), (0, 0), (0, LANES - MLA_NOPE))).reshape(
        MLA_KV_LORA, MLA_HEADS * LANES)
    wuv = kv[:, :, MLA_NOPE:].reshape(MLA_KV_LORA, MLA_HEADS * MLA_V)
    return [w.astype(BF16) for w in (wc, wd, ws, wuq, wuk, wuv)]


def kernel(x, p, g_ffn1, w1_a, w3_a, w2_a, g_mix, w_in, g_cq, g_ckv, w_uq, w_ukv, lambda_q1, lambda_k1, lambda_q2, lambda_k2, g_subln, w_o_mla, w_o_diff, w_o_sb, w_branch_gate, w_out, g_ffn2, w1_b, w3_b, w2_b, g_ple, w_ple_gate, w_ple_proj, g_final):
    b, s, d = x.shape
    t = b * s
    assert d == D_MODEL and t % ROW_TILE == 0 and s % PROJ_TILE == 0 and s % ATT_BLOCK == 0
    cos_t, sin_a, sin_b = _rope_tables(s)
    slopes = jnp.asarray(np.exp2(-8.0 * np.arange(1, DIFF_HEADS + 1, dtype=np.float32) / DIFF_HEADS), F32)
    row = lambda v: v.reshape(1, -1)
    w1_a, w3_a, w2_a, w1_b, w3_b, w2_b = map(_to_bf16, (w1_a, w3_a, w2_a, w1_b, w3_b, w2_b))
    w_uq, w_ukv = map(_to_bf16, (w_uq, w_ukv))
    w_in = w_in.astype(BF16)
    w_branch_gate, w_o_mla, w_o_diff, w_o_sb, w_out = map(
        _to_bf16, (w_branch_gate, w_o_mla, w_o_diff, w_o_sb, w_out))
    w_ple_gate, w_ple_proj = map(_to_bf16, (w_ple_gate, w_ple_proj))

    h = x.reshape(t, d)
    for i in range(DEPTH):
        lam_init = 0.8 - 0.6 * math.exp(-0.3 * i)
        h = _ffn(h, row(g_ffn1[i]), w1_a, w3_a, w2_a, i)
        wc, wd, ws, wuq, wuk, wuv = _layer_weights(w_in[i], w_uq[i], w_ukv[i])
        qm, km, vm, qd, kd, vd, qs, ks, vs = _proj(
            h.reshape(b, s, d), row(g_mix[i]), wc, wd, ws, row(g_cq[i]), row(g_ckv[i]),
            wuq, wuk, wuv, cos_t, sin_a, sin_b)
        y_mla = _mla(qm, km, vm).reshape(t, -1)
        lam_p = jnp.stack([lambda_q1[i], lambda_k1[i], lambda_q2[i], lambda_k2[i]])
        y_diff = _diff(qd, kd, vd, slopes, lam_p, row(g_subln[i]), lam_init).reshape(t, -1)
        y_sb = _sb(qs, ks, vs).reshape(t, -1)
        h = _merge(h, row(g_mix[i]), y_mla, y_diff, y_sb, w_branch_gate, w_o_mla, w_o_diff, w_o_sb, w_out, i)
        h = _ffn_ple(h, row(g_ffn2[i]), w1_b, w3_b, w2_b, row(g_ple[i]), p.reshape(DEPTH * t, PLE_DIM),
                     w_ple_gate, w_ple_proj, row(g_final), i, final_norm=(i == DEPTH - 1))
    return h.reshape(b, s, d)
```
